```python
import math
import jax
import jax.numpy as jnp
from jax import lax
import numpy as np

D_MODEL = 2048
BATCH = 4
SEQ = 4096
DEPTH = 2

HEAD_DIM = 128
N_HEADS = D_MODEL // HEAD_DIM
N_HEADS_A = N_HEADS // 2
N_HEADS_B = N_HEADS - N_HEADS_A
DILATED_BRANCHES = ((128, 1), (512, 4), (2048, 16))
MOBA_BLOCK = 256
MOBA_TOPK = 3
MOBA_QCHUNK = 32
SB_QBLOCK = 128
REL_BUCKETS = 32
REL_MAX_DIST = 2048
D_FF = 5632
FFN_HALF = 0.5
RMS_EPS = 1e-6
N_EVEN = (DEPTH + 1) // 2
N_ODD = DEPTH // 2

kernel_name = 'hybrid_dilated_moba_stickbreaking_macaron'


def rms_norm(x, g):
    xf = x.astype(jnp.float32)
    y = xf * lax.rsqrt(jnp.mean(xf * xf, axis=-1, keepdims=True) + RMS_EPS)
    return (y * g.astype(jnp.float32)).astype(x.dtype)


def swiglu(x, w_gate, w_up, w_down):
    return (jax.nn.silu(x @ w_gate) * (x @ w_up)) @ w_down


def rel_bucket(dist):
    max_exact = REL_BUCKETS // 2
    d = jnp.maximum(dist, 0)
    df = jnp.maximum(d, 1).astype(jnp.float32)
    large = max_exact + (jnp.log(df / max_exact) / math.log(REL_MAX_DIST / max_exact)
                         * (REL_BUCKETS - max_exact)).astype(jnp.int32)
    large = jnp.minimum(large, REL_BUCKETS - 1)
    return jnp.where(d < max_exact, d, large)


def dilated_branch(q, k, v, tab, window, dilation):
    Bb, H, S, hd = q.shape
    W = window // dilation
    L = S // dilation
    nb = -(-L // W)
    Lp = nb * W

    def to_sub(t):
        t = t.reshape(Bb, H, L, dilation, hd).transpose(0, 1, 3, 2, 4)
        return jnp.pad(t, ((0, 0), (0, 0), (0, 0), (0, Lp - L), (0, 0)))

    def band(t):
        tp = jnp.pad(to_sub(t), ((0, 0), (0, 0), (0, 0), (W, 0), (0, 0))).reshape(Bb, H, dilation, nb + 1, W, hd)
        return jnp.concatenate([tp[:, :, :, :-1], tp[:, :, :, 1:]], axis=4)

    qs = to_sub(q).reshape(Bb, H, dilation, nb, W, hd)
    kb, vb = band(k), band(v)
    i = jnp.arange(W)[:, None]
    j = jnp.arange(2 * W)[None, :]
    rel = W + i - j
    blk_start = jnp.arange(nb)[:, None, None] * W
    valid = (rel >= 0) & (rel <= W) & (blk_start - W + j >= 0)
    bias = tab[:, rel_bucket(rel * dilation)]
    s = jnp.einsum('bhrnqd,bhrnkd->bhrnqk', qs, kb) * (HEAD_DIM ** -0.5) + bias[:, None, None]
    s = jnp.where(valid, s, -jnp.inf)
    m = jnp.max(s, axis=-1, keepdims=True)
    p = jnp.exp(s - m)
    den = jnp.sum(p, axis=-1)
    o = jnp.einsum('bhrnqk,bhrnkd->bhrnqd', p, vb) / den[..., None]
    lse = m[..., 0] + jnp.log(den)

    def from_sub(t, tail):
        t = t.reshape(Bb, H, dilation, Lp, *tail)[:, :, :, :L]
        return jnp.moveaxis(t, 2, 3).reshape(Bb, H, S, *tail)

    return from_sub(o, (hd,)), from_sub(lse, ())


def dilated_mixture(q, k, v, tab):
    results = [dilated_branch(q, k, v, tab, w, r) for (w, r) in DILATED_BRANCHES]
    outs = jnp.stack([o for (o, _) in results])
    lses = jnp.stack([l for (_, l) in results])
    alpha = jax.nn.softmax(lses, axis=0)
    return jnp.sum(alpha[..., None] * outs, axis=0)


def moba_attention(q, k, v, tab):
    Bb, H, S, hd = q.shape
    nblk = -(-S // MOBA_BLOCK)
    Sp = nblk * MOBA_BLOCK
    pad = ((0, 0), (0, 0), (0, Sp - S), (0, 0))
    kp, vp = jnp.pad(k, pad), jnp.pad(v, pad)
    kb = kp.reshape(Bb, H, nblk, MOBA_BLOCK, hd)
    vb = vp.reshape(Bb, H, nblk, MOBA_BLOCK, hd)
    own = jnp.arange(S) // MOBA_BLOCK
    gate = jnp.einsum('bhsd,bhnd->bhsn', q, jnp.mean(kb, axis=3))
    gate = jnp.where(jnp.arange(nblk)[None, :] < own[:, None], gate, -jnp.inf)
    n_sel = min(MOBA_TOPK, nblk)
    _, sel = lax.top_k(gate, n_sel)
    sel_ok = sel < own[:, None]
    nC = S // MOBA_QCHUNK

    def chunks(t):
        return jnp.moveaxis(t.reshape(Bb, H, nC, MOBA_QCHUNK, *t.shape[3:]), 2, 0)

    bi = jnp.arange(Bb)[:, None, None, None]
    hi = jnp.arange(H)[None, :, None, None]
    hi5 = jnp.arange(H)[None, :, None, None, None]
    offs = jnp.arange(MOBA_BLOCK)
    scale = HEAD_DIM ** -0.5

    def chunk_attn(args):
        qc, selc, okc, c = args
        qpos = c * MOBA_QCHUNK + jnp.arange(MOBA_QCHUNK)
        own_start = (c * MOBA_QCHUNK // MOBA_BLOCK) * MOBA_BLOCK
        k_sel = kb[bi, hi, selc]
        v_sel = vb[bi, hi, selc]
        k_own = lax.dynamic_slice_in_dim(kp, own_start, MOBA_BLOCK, axis=2)
        v_own = lax.dynamic_slice_in_dim(vp, own_start, MOBA_BLOCK, axis=2)
        dist_sel = qpos[:, None, None] - (selc[..., None] * MOBA_BLOCK + offs)
        s_sel = jnp.einsum('bhqd,bhqnkd->bhqnk', qc, k_sel) * scale + tab[hi5, rel_bucket(dist_sel)]
        s_sel = jnp.where(okc[..., None], s_sel, -jnp.inf)
        dist_own = qpos[:, None] - (own_start + offs)[None, :]
        s_own = jnp.einsum('bhqd,bhkd->bhqk', qc, k_own) * scale + tab[:, rel_bucket(dist_own)]
        s_own = jnp.where(dist_own >= 0, s_own, -jnp.inf)
        n_k = n_sel * MOBA_BLOCK
        p = jax.nn.softmax(jnp.concatenate([s_sel.reshape(Bb, H, MOBA_QCHUNK, n_k), s_own], axis=-1), axis=-1)
        p_sel = p[..., :n_k].reshape(Bb, H, MOBA_QCHUNK, n_sel, MOBA_BLOCK)
        return (jnp.einsum('bhqnk,bhqnkd->bhqd', p_sel, v_sel)
                + jnp.einsum('bhqk,bhkd->bhqd', p[..., n_k:], v_own))

    out = lax.map(chunk_attn, (chunks(q), chunks(sel), chunks(sel_ok), jnp.arange(nC)))
    return jnp.moveaxis(out, 0, 2).reshape(Bb, H, S, hd)


def stick_breaking_attention(q, k, v):
    Bb, H, S, hd = q.shape
    nQ = S // SB_QBLOCK
    kpos = jnp.arange(S)
    scale = HEAD_DIM ** -0.5

    def block(args):
        qc, c = args
        qpos = c * SB_QBLOCK + jnp.arange(SB_QBLOCK)
        past = kpos[None, :] < qpos[:, None]
        z = jnp.einsum('bhqd,bhkd->bhqk', qc, k) * scale
        log_keep = jnp.where(past, jax.nn.log_sigmoid(-z), 0.0)
        between = lax.cumsum(log_keep, axis=3, reverse=True) - log_keep
        log_w = jnp.where(past, jax.nn.log_sigmoid(z) + between, -jnp.inf)
        return jnp.einsum('bhqk,bhkd->bhqd', jnp.exp(log_w), v)

    qb = jnp.moveaxis(q.reshape(Bb, H, nQ, SB_QBLOCK, hd), 2, 0)
    out = lax.map(block, (qb, jnp.arange(nQ)))
    return jnp.moveaxis(out, 0, 2).reshape(Bb, H, S, hd)


def split_qkv(u, w_qkv):
    Bb, S, _ = u.shape
    qkv = (u @ w_qkv).reshape(Bb, S, 3, N_HEADS, HEAD_DIM).astype(jnp.float32)
    return [qkv[:, :, j].transpose(0, 2, 1, 3) for j in range(3)]


def merge_heads(o, w_out, dtype):
    Bb, H, S, hd = o.shape
    return o.transpose(0, 2, 1, 3).reshape(Bb, S, H * hd).astype(dtype) @ w_out


def mix_even(u, w_qkv, w_out, rel_bias):
    q, k, v = split_qkv(u, w_qkv)
    tab = rel_bias.astype(jnp.float32).T
    a = slice(0, N_HEADS_A)
    b = slice(N_HEADS_A, N_HEADS)
    o_a = dilated_mixture(q[:, a], k[:, a], v[:, a], tab[a])
    o_b = moba_attention(q[:, b], k[:, b], v[:, b], tab[b])
    return merge_heads(jnp.concatenate([o_a, o_b], axis=1), w_out, u.dtype)


def mix_odd(u, w_qkv, w_out):
    q, k, v = split_qkv(u, w_qkv)
    return merge_heads(stick_breaking_attention(q, k, v), w_out, u.dtype)


def setup_inputs(seed: int = 0) -> dict:
    key = jax.random.key(seed)
    ks = jax.random.split(key, 12)
    D = D_MODEL
    nrm = jax.random.normal
    x = nrm(ks[0], (BATCH, SEQ, D), jnp.float32)
    ln_gains = 1.0 + 0.02 * nrm(ks[1], (DEPTH, 3, D), jnp.float32)
    ffn_w_gate = nrm(ks[2], (DEPTH, 2, D, D_FF), jnp.float32) * D ** -0.5
    ffn_w_up = nrm(ks[3], (DEPTH, 2, D, D_FF), jnp.float32) * D ** -0.5
    ffn_w_down = nrm(ks[4], (DEPTH, 2, D_FF, D), jnp.float32) * D_FF ** -0.5
    w_qkv_even = nrm(ks[5], (N_EVEN, D, 3 * D), jnp.float32) * D ** -0.5
    w_out_even = nrm(ks[6], (N_EVEN, D, D), jnp.float32) * D ** -0.5
    w_qkv_odd = nrm(ks[7], (N_ODD, D, 3 * D), jnp.float32) * D ** -0.5
    w_out_odd = nrm(ks[8], (N_ODD, D, D), jnp.float32) * D ** -0.5
    rel_bias = 0.5 * nrm(ks[9], (REL_BUCKETS, N_HEADS), jnp.float32)
    final_gain = 1.0 + 0.02 * nrm(ks[10], (D,), jnp.float32)
    return {'x': x, 'ln_gains': ln_gains, 'ffn_w_gate': ffn_w_gate, 'ffn_w_up': ffn_w_up,
            'ffn_w_down': ffn_w_down, 'w_qkv_even': w_qkv_even, 'w_out_even': w_out_even,
            'w_qkv_odd': w_qkv_odd, 'w_out_odd': w_out_odd, 'rel_bias': rel_bias,
            'final_gain': final_gain}


def reference(x, ln_gains, ffn_w_gate, ffn_w_up, ffn_w_down, w_qkv_even, w_out_even,
              w_qkv_odd, w_out_odd, rel_bias, final_gain):
    h = x
    for i in range(DEPTH):
        h = h + FFN_HALF * swiglu(rms_norm(h, ln_gains[i, 0]), ffn_w_gate[i, 0], ffn_w_up[i, 0], ffn_w_down[i, 0])
        u = rms_norm(h, ln_gains[i, 1])
        if i % 2 == 0:
            h = h + mix_even(u, w_qkv_even[i // 2], w_out_even[i // 2], rel_bias)
        else:
            h = h + mix_odd(u, w_qkv_odd[i // 2], w_out_odd[i // 2])
        h = h + FFN_HALF * swiglu(rms_norm(h, ln_gains[i, 2]), ffn_w_gate[i, 1], ffn_w_up[i, 1], ffn_w_down[i, 1])
    return rms_norm(h, final_gain)
```

```python
import functools
import math

import jax
import jax.numpy as jnp
from jax import lax
from jax.experimental import pallas as pl
from jax.experimental.pallas import tpu as pltpu

F32 = jnp.float32
BF16 = jnp.bfloat16

HEAD_DIM = 128
N_HEADS = 16
N_HEADS_A = 8
DILATIONS = (1, 4, 16)
DIL_W = 128
MOBA_BLOCK = 256
MOBA_TOPK = 3
MOBA_GATE_LANES = 128
REL_BUCKETS = 32
REL_MAX_DIST = 2048
FFN_HALF = 0.5
RMS_EPS = 1e-6
NEG = -1e30
SB_TQ = 256
SB_TK = 128
SB_DEAD = -105.0
V7X_VMEM_LIMIT = 56 * 1024 * 1024


def _cparams(sem):
    return pltpu.CompilerParams(dimension_semantics=sem, vmem_limit_bytes=V7X_VMEM_LIMIT)


def _rms(x, g):
    ms = jnp.mean(x * x, axis=-1, keepdims=True)
    return x * lax.rsqrt(ms + RMS_EPS) * g


def _dot(a, b):
    return jnp.dot(a, b, preferred_element_type=F32)


def _dot_nt(a, b):
    return lax.dot_general(a, b, (((1,), (1,)), ((), ())), preferred_element_type=F32)


def _ffn_kernel(h_ref, g_ref, wg_ref, wu_ref, wd_ref, *rest, nf, dn, final):
    if final:
        fg_ref, o_ref, n_ref = rest
    else:
        o_ref, n_ref = rest
    f = pl.program_id(1)

    @pl.when(f == 0)
    def _():
        n_ref[...] = _rms(h_ref[...], g_ref[...]).astype(BF16)

    n = n_ref[...]
    gate = _dot(n, wg_ref[...])
    up = _dot(n, wu_ref[...])
    a = (gate * (1.0 / (1.0 + jnp.exp(-gate))) * up).astype(BF16)
    d = o_ref.shape[1]
    for c in range(d // dn):
        sl = slice(c * dn, (c + 1) * dn)
        part = _dot(a, wd_ref[:, sl])

        @pl.when(f == 0)
        def _():
            o_ref[:, sl] = part

        @pl.when(f > 0)
        def _():
            o_ref[:, sl] += part

    @pl.when(f == nf - 1)
    def _():
        y = h_ref[...] + FFN_HALF * o_ref[...]
        if final:
            y = _rms(y, fg_ref[...])
        o_ref[...] = y


def _ffn(h, gain, wg, wu, wd, final_gain=None, *, tm=512, tf=512, dn=512):
    m, d = h.shape
    dff = wg.shape[1]
    nf = dff // tf
    final = final_gain is not None
    in_specs = [
        pl.BlockSpec((tm, d), lambda i, f: (i, 0)),
        pl.BlockSpec((1, d), lambda i, f: (0, 0)),
        pl.BlockSpec((d, tf), lambda i, f: (0, f)),
        pl.BlockSpec((d, tf), lambda i, f: (0, f)),
        pl.BlockSpec((tf, d), lambda i, f: (f, 0)),
    ]
    args = [h, gain.reshape(1, d), wg, wu, wd]
    if final:
        in_specs.append(pl.BlockSpec((1, d), lambda i, f: (0, 0)))
        args.append(final_gain.reshape(1, d))
    return pl.pallas_call(
        functools.partial(_ffn_kernel, nf=nf, dn=dn, final=final),
        grid=(m // tm, nf),
        in_specs=in_specs,
        out_specs=pl.BlockSpec((tm, d), lambda i, f: (i, 0)),
        out_shape=jax.ShapeDtypeStruct((m, d), F32),
        scratch_shapes=[pltpu.VMEM((tm, d), BF16)],
        compiler_params=_cparams(("parallel", "arbitrary")),
        name="ffn_final" if final else "ffn",
    )(*args)


def _qkv_kernel(h_ref, g_ref, w_ref, o_ref, n_ref, *, q_blocks, scale):
    j = pl.program_id(1)

    @pl.when(j == 0)
    def _():
        n_ref[...] = _rms(h_ref[...], g_ref[...]).astype(BF16)

    acc = _dot(n_ref[...], w_ref[...])
    acc = acc * jnp.where(j < q_blocks, scale, 1.0).astype(F32)
    for c in range(o_ref.shape[0]):
        o_ref[c] = acc[:, c * HEAD_DIM:(c + 1) * HEAD_DIM].astype(BF16)


def _qkv(h, gain, w, *, tm=512, tn=1024):
    m, d = h.shape
    n3 = w.shape[1]
    hpb = tn // HEAD_DIM
    return pl.pallas_call(
        functools.partial(_qkv_kernel, q_blocks=(n3 // 3) // tn, scale=HEAD_DIM ** -0.5),
        grid=(m // tm, n3 // tn),
        in_specs=[
            pl.BlockSpec((tm, d), lambda i, j: (i, 0)),
            pl.BlockSpec((1, d), lambda i, j: (0, 0)),
            pl.BlockSpec((d, tn), lambda i, j: (0, j)),
        ],
        out_specs=pl.BlockSpec((hpb, tm, HEAD_DIM), lambda i, j: (j, i, 0)),
        out_shape=jax.ShapeDtypeStruct((n3 // HEAD_DIM, m, HEAD_DIM), BF16),
        scratch_shapes=[pltpu.VMEM((tm, d), BF16)],
        compiler_params=_cparams(("parallel", "arbitrary")),
        name="qkv",
    )(h, gain.reshape(1, d), w)


def _oproj_kernel(h_ref, *rest, n_in):
    a_refs = rest[:n_in]
    w_ref = rest[n_in]
    o_ref = rest[n_in + 1]
    acc = h_ref[...]
    off = 0
    for a_ref in a_refs:
        k = a_ref.shape[1]
        acc = acc + _dot(a_ref[...], w_ref[off:off + k, :])
        off += k
    o_ref[...] = acc


def _oproj(h, attn_parts, w, *, tm=512):
    m, d = h.shape
    in_specs = [pl.BlockSpec((tm, d), lambda i: (i, 0))]
    for a in attn_parts:
        in_specs.append(pl.BlockSpec((tm, a.shape[1]), lambda i: (i, 0)))
    in_specs.append(pl.BlockSpec(w.shape, lambda i: (0, 0)))
    return pl.pallas_call(
        functools.partial(_oproj_kernel, n_in=len(attn_parts)),
        grid=(m // tm,),
        in_specs=in_specs,
        out_specs=pl.BlockSpec((tm, d), lambda i: (i, 0)),
        out_shape=jax.ShapeDtypeStruct((m, d), F32),
        compiler_params=_cparams(("parallel",)),
        name="oproj",
    )(h, *attn_parts, w)


def _bucket_bias(dist, tab_ref, head):
    max_exact = REL_BUCKETS // 2
    d = jnp.maximum(dist, 0)
    df = jnp.maximum(d, 1).astype(F32)
    large = max_exact + (jnp.log(df / max_exact) / math.log(REL_MAX_DIST / max_exact)
                         * (REL_BUCKETS - max_exact)).astype(jnp.int32)
    large = jnp.minimum(large, REL_BUCKETS - 1)
    bucket = jnp.where(d < max_exact, d, large)
    out = jnp.full(dist.shape, tab_ref[0, head], F32)
    for b in range(1, REL_BUCKETS):
        out = jnp.where(bucket == b, tab_ref[b, head], out)
    return out


def _dilated_kernel(tab_ref, *refs, seq, head0):
    nbr = len(DILATIONS)
    qkv_refs = refs[:3 * nbr]
    o_ref = refs[3 * nbr]
    bias_ref, num_ref, lse_ref = refs[3 * nbr + 1:]
    h = pl.program_id(0)
    b = pl.program_id(1)
    w = DIL_W

    @pl.when(b == 0)
    def _():
        i = lax.broadcasted_iota(jnp.int32, (w, 2 * w), 0)
        j = lax.broadcasted_iota(jnp.int32, (w, 2 * w), 1)
        rel = w + i - j
        ok = (rel >= 0) & (rel <= w)
        for br, r in enumerate(DILATIONS):
            bias = _bucket_bias(rel * r, tab_ref, head0 + h)
            bias_ref[br, 1] = jnp.where(ok, bias, NEG)
            bias_ref[br, 0] = jnp.where(ok & (j >= w), bias, NEG)

    for br, r in enumerate(DILATIONS):
        q_ref, k_ref, v_ref = qkv_refs[3 * br:3 * br + 3]
        nb = seq // r // w
        for c in range(r):
            cs = slice(c * HEAD_DIM, (c + 1) * HEAD_DIM)

            def body(n, carry, q_ref=q_ref, k_ref=k_ref, v_ref=v_ref, cs=cs, br=br, r=r, c=c):
                own = pl.multiple_of(n * w, w)
                prev = pl.multiple_of(jnp.maximum(n - 1, 0) * w, w)
                qb = q_ref[0, 0, pl.ds(own, w), cs]
                kb = jnp.concatenate([k_ref[0, 0, pl.ds(prev, w), cs], k_ref[0, 0, pl.ds(own, w), cs]], axis=0)
                vb = jnp.concatenate([v_ref[0, 0, pl.ds(prev, w), cs], v_ref[0, 0, pl.ds(own, w), cs]], axis=0)
                s = _dot_nt(qb, kb) + bias_ref[br, jnp.minimum(n, 1)]
                m = jnp.max(s, axis=-1, keepdims=True)
                p = jnp.exp(s - m)
                den = jnp.sum(p, axis=-1, keepdims=True)
                o = _dot(p.astype(BF16), vb) / den
                lse = m + jnp.log(den)
                rows = pl.ds(n * (w * r) + c, w, stride=r) if r > 1 else pl.ds(own, w)
                num_ref[br, rows, :] = o
                lse_ref[br, rows, :] = jnp.broadcast_to(lse, (w, HEAD_DIM))
                return carry

            lax.fori_loop(0, nb, body, 0)

    ch = 512

    def combine(t, carry):
        rows = pl.ds(pl.multiple_of(t * ch, ch), ch)
        ls = [lse_ref[br, rows, :] for br in range(nbr)]
        m = functools.reduce(jnp.maximum, ls)
        ws = [jnp.exp(l - m) for l in ls]
        tot = functools.reduce(lambda x, y: x + y, ws)
        acc = functools.reduce(lambda x, y: x + y, [wt * num_ref[br, rows, :] for br, wt in enumerate(ws)])
        o_ref[0, rows, :] = (acc / tot).astype(o_ref.dtype)
        return carry

    lax.fori_loop(0, seq // ch, combine, 0)


def _dilated(qkv, rel_bias, *, batch, seq, head0=0, n_heads=N_HEADS_A):
    nh_all = qkv.shape[0] // 3
    in_specs = [pl.BlockSpec(memory_space=pltpu.SMEM)]
    args = [rel_bias]
    for r in DILATIONS:
        view = qkv.reshape(3 * nh_all, batch, seq // r, r * HEAD_DIM)
        for j in range(3):
            in_specs.append(pl.BlockSpec((1, 1, seq // r, r * HEAD_DIM),
                                         lambda h, b, j=j: (j * nh_all + head0 + h, b, 0, 0)))
            args.append(view)
    nbr = len(DILATIONS)
    return pl.pallas_call(
        functools.partial(_dilated_kernel, seq=seq, head0=head0),
        grid=(n_heads, batch),
        in_specs=in_specs,
        out_specs=pl.BlockSpec((1, seq, HEAD_DIM), lambda h, b: (b, 0, h)),
        out_shape=jax.ShapeDtypeStruct((batch, seq, n_heads * HEAD_DIM), BF16),
        scratch_shapes=[
            pltpu.VMEM((nbr, 2, DIL_W, 2 * DIL_W), F32),
            pltpu.VMEM((nbr, seq, HEAD_DIM), F32),
            pltpu.VMEM((nbr, seq, HEAD_DIM), F32),
        ],
        compiler_params=_cparams(("arbitrary", "arbitrary")),
        name="dilated",
    )(*args)


def _moba_kernel(tab_ref, q_ref, k_ref, v_ref, o_ref, bias_ref, kbar_ref, pen_ref, m_ref, l_ref, acc_ref,
                 *, seq, head0):
    blk = MOBA_BLOCK
    nblk = seq // blk
    h = pl.program_id(0)
    b = pl.program_id(1)
    i = pl.program_id(2)

    @pl.when((b == 0) & (i == 0))
    def _():
        r = lax.broadcasted_iota(jnp.int32, (blk, blk), 0)
        c = lax.broadcasted_iota(jnp.int32, (blk, blk), 1)

        def fill(dlt, carry):
            dist = dlt * blk + r - c
            bias_ref[dlt] = jnp.where(dist >= 0, _bucket_bias(dist, tab_ref, head0 + h), NEG)
            return carry

        lax.fori_loop(0, nblk, fill, 0)

    @pl.when(i == 0)
    def _():
        kbar_ref[...] = jnp.zeros(kbar_ref.shape, F32)
        for n in range(nblk):
            kn = k_ref[0, 0, n * blk:(n + 1) * blk, :].astype(F32)
            kbar_ref[n:n + 1, :] = jnp.sum(kn, axis=0, keepdims=True) * (1.0 / blk)

    q = q_ref[0, 0]

    kbar = kbar_ref[...]
    k1 = kbar.astype(BF16)
    r1 = kbar - k1.astype(F32)
    k2 = r1.astype(BF16)
    k3 = (r1 - k2.astype(F32)).astype(BF16)
    gate = _dot_nt(q, k1) + _dot_nt(q, k2) + _dot_nt(q, k3)
    lane = lax.broadcasted_iota(jnp.int32, gate.shape, 1)
    gate = jnp.where(lane < i, gate, -jnp.inf)
    pen = jnp.full(gate.shape, NEG, F32)
    for _ in range(min(MOBA_TOPK, nblk)):
        best = jnp.max(gate, axis=-1, keepdims=True)
        idx = jnp.min(jnp.where(gate == best, lane, MOBA_GATE_LANES), axis=-1, keepdims=True)
        pen = jnp.where((lane == idx) & (best > -jnp.inf), 0.0, pen)
        gate = jnp.where(lane == idx, -jnp.inf, gate)
    for n in range(nblk):
        pen_ref[n] = jnp.broadcast_to(pen[:, n:n + 1], (blk, HEAD_DIM))

    def attend(s, vb):
        m_old = m_ref[...]
        m_new = jnp.maximum(m_old, jnp.max(s, axis=-1, keepdims=True))
        alpha = jnp.exp(m_old - m_new)
        p = jnp.exp(s - m_new)
        l_ref[...] = alpha * l_ref[...] + jnp.sum(p, axis=-1, keepdims=True)
        acc_ref[...] = alpha * acc_ref[...] + _dot(p.astype(BF16), vb)
        m_ref[...] = m_new

    own = pl.multiple_of(i * blk, blk)
    m_ref[...] = jnp.full(m_ref.shape, NEG, F32)
    l_ref[...] = jnp.zeros(l_ref.shape, F32)
    acc_ref[...] = jnp.zeros(acc_ref.shape, F32)
    attend(_dot_nt(q, k_ref[0, 0, pl.ds(own, blk), :]) + bias_ref[0], v_ref[0, 0, pl.ds(own, blk), :])

    def body(n, carry):
        rows = pl.ds(pl.multiple_of(n * blk, blk), blk)
        pn = pen_ref[n]
        s = _dot_nt(q, k_ref[0, 0, rows, :]) + bias_ref[i - n] + jnp.concatenate([pn] * (blk // HEAD_DIM), axis=1)
        attend(s, v_ref[0, 0, rows, :])
        return carry

    lax.fori_loop(0, i, body, 0)
    o_ref[0] = (acc_ref[...] / l_ref[...]).astype(o_ref.dtype)


def _moba(qkv, rel_bias, *, batch, seq, head0=N_HEADS_A, n_heads=N_HEADS - N_HEADS_A):
    nh_all = qkv.shape[0] // 3
    blk = MOBA_BLOCK
    nblk = seq // blk
    view = qkv.reshape(3 * nh_all, batch, seq, HEAD_DIM)
    return pl.pallas_call(
        functools.partial(_moba_kernel, seq=seq, head0=head0),
        grid=(n_heads, batch, nblk),
        in_specs=[
            pl.BlockSpec(memory_space=pltpu.SMEM),
            pl.BlockSpec((1, 1, blk, HEAD_DIM), lambda h, b, i: (head0 + h, b, i, 0)),
            pl.BlockSpec((1, 1, seq, HEAD_DIM), lambda h, b, i: (nh_all + head0 + h, b, 0, 0)),
            pl.BlockSpec((1, 1, seq, HEAD_DIM), lambda h, b, i: (2 * nh_all + head0 + h, b, 0, 0)),
        ],
        out_specs=pl.BlockSpec((1, blk, HEAD_DIM), lambda h, b, i: (b, i, h)),
        out_shape=jax.ShapeDtypeStruct((batch, seq, n_heads * HEAD_DIM), BF16),
        scratch_shapes=[
            pltpu.VMEM((nblk, blk, blk), F32),
            pltpu.VMEM((MOBA_GATE_LANES, HEAD_DIM), F32),
            pltpu.VMEM((nblk, blk, HEAD_DIM), F32),
            pltpu.VMEM((blk, 1), F32),
            pltpu.VMEM((blk, 1), F32),
            pltpu.VMEM((blk, HEAD_DIM), F32),
        ],
        compiler_params=_cparams(("arbitrary", "arbitrary", "arbitrary")),
        name="moba",
    )(rel_bias, view, view, view)


def _sb_kernel(q_ref, k_ref, v_ref, o_ref, c_ref, acc_ref):
    tq, tk = SB_TQ, SB_TK
    i = pl.program_id(2)
    q = q_ref[0, 0]
    rows = lax.broadcasted_iota(jnp.int32, (tq, tk), 0)
    cols = lax.broadcasted_iota(jnp.int32, (tq, tk), 1)
    uj = lax.broadcasted_iota(jnp.int32, (2 * tk, tk), 0)
    us = lax.broadcasted_iota(jnp.int32, (2 * tk, tk), 1)
    suffix = jnp.where((uj % tk) > us, 1.0, 0.0).astype(BF16)

    c_ref[...] = jnp.zeros(c_ref.shape, F32)
    acc_ref[...] = jnp.zeros(acc_ref.shape, F32)

    def block(j, masked):
        krows = pl.ds(pl.multiple_of(j * tk, tk), tk)
        z = _dot_nt(q, k_ref[0, 0, krows, :])
        sp = jnp.maximum(z, 0.0) + jnp.log(1.0 + jnp.exp(-jnp.abs(z)))
        lk = -sp
        if masked:
            past = (j * tk + cols) < (i * tq + rows)
            lk = jnp.where(past, lk, 0.0)
        hi = lk.astype(BF16)
        lo = (lk - hi.astype(F32)).astype(BF16)
        between = _dot(jnp.concatenate([hi, lo], axis=1), suffix) + c_ref[...]
        logw = (z - sp) + between
        if masked:
            logw = jnp.where(past, logw, -jnp.inf)
        p = jnp.exp(logw).astype(BF16)
        acc_ref[...] += _dot(p, v_ref[0, 0, krows, :])
        c_ref[...] += jnp.sum(lk, axis=-1, keepdims=True)

    ndiag = tq // tk
    for d in range(ndiag):
        block(i * ndiag + (ndiag - 1 - d), True)

    def cond(state):
        j, cmax = state
        return (j >= 0) & (cmax > SB_DEAD)

    def body(state):
        j, _ = state
        block(j, False)
        return j - 1, jnp.max(c_ref[...])

    lax.while_loop(cond, body, (i * ndiag - 1, jnp.max(c_ref[...])))
    o_ref[0] = acc_ref[...].astype(o_ref.dtype)


def _stickbreaking(qkv, *, batch, seq):
    nh = qkv.shape[0] // 3
    view = qkv.reshape(3 * nh, batch, seq, HEAD_DIM)
    return pl.pallas_call(
        _sb_kernel,
        grid=(batch, nh, seq // SB_TQ),
        in_specs=[
            pl.BlockSpec((1, 1, SB_TQ, HEAD_DIM), lambda b, h, i: (h, b, i, 0)),
            pl.BlockSpec((1, 1, seq, HEAD_DIM), lambda b, h, i: (nh + h, b, 0, 0)),
            pl.BlockSpec((1, 1, seq, HEAD_DIM), lambda b, h, i: (2 * nh + h, b, 0, 0)),
        ],
        out_specs=pl.BlockSpec((1, SB_TQ, HEAD_DIM), lambda b, h, i: (b, i, h)),
        out_shape=jax.ShapeDtypeStruct((batch, seq, nh * HEAD_DIM), BF16),
        scratch_shapes=[pltpu.VMEM((SB_TQ, 1), F32), pltpu.VMEM((SB_TQ, HEAD_DIM), F32)],
        compiler_params=_cparams(("parallel", "parallel", "arbitrary")),
        name="stickbreaking",
    )(view, view, view)


def kernel(x, ln_gains, ffn_w_gate, ffn_w_up, ffn_w_down, w_qkv_even, w_out_even, w_qkv_odd, w_out_odd,
           rel_bias, final_gain):
    batch, seq, d = x.shape
    depth = ln_gains.shape[0]
    m = batch * seq
    wg, wu, wd = (w.astype(BF16) for w in (ffn_w_gate, ffn_w_up, ffn_w_down))
    wqkv = (w_qkv_even.astype(BF16), w_qkv_odd.astype(BF16))
    wout = (w_out_even.astype(BF16), w_out_odd.astype(BF16))
    rel_bias = rel_bias.astype(F32)

    h = x.reshape(m, d)
    for i in range(depth):
        h = _ffn(h, ln_gains[i, 0], wg[i, 0], wu[i, 0], wd[i, 0])
        qkv = _qkv(h, ln_gains[i, 1], wqkv[i % 2][i // 2])
        if i % 2 == 0:
            parts = [_dilated(qkv, rel_bias, batch=batch, seq=seq), _moba(qkv, rel_bias, batch=batch, seq=seq)]
        else:
            parts = [_stickbreaking(qkv, batch=batch, seq=seq)]
        parts = [p.reshape(m, p.shape[-1]) for p in parts]
        h = _oproj(h, parts, wout[i % 2][i // 2])
        last = i == depth - 1
        h = _ffn(h, ln_gains[i, 2], wg[i, 1], wu[i, 1], wd[i, 1], final_gain if last else None)
    return h.reshape(batch, seq, d)
```

```python
import functools
import math

import jax
import jax.numpy as jnp
from jax import lax
from jax.experimental import pallas as pl
from jax.experimental.pallas import tpu as pltpu

F32 = jnp.float32
BF16 = jnp.bfloat16

HEAD_DIM = 128
N_HEADS = 16
N_HEADS_A = 8
DILATIONS = (1, 4, 16)
DIL_W = 128
MOBA_BLOCK = 256
MOBA_TOPK = 3
MOBA_GROUP = 4
MOBA_MASK = -(2.0 ** 100)
REL_BUCKETS = 32
REL_MAX_DIST = 2048
FFN_HALF = 0.5
RMS_EPS = 1e-6
NEG = -1e30
SB_TQ = 256
SB_TK = 128
SB_DEAD = -105.0
V7X_VMEM_LIMIT = 56 * 1024 * 1024


def _cparams(sem):
    return pltpu.CompilerParams(dimension_semantics=sem, vmem_limit_bytes=V7X_VMEM_LIMIT)


def _rms(x, g):
    ms = jnp.mean(x * x, axis=-1, keepdims=True)
    return x * lax.rsqrt(ms + RMS_EPS) * g


def _dot(a, b):
    return jnp.dot(a, b, preferred_element_type=F32)


def _dot_nt(a, b):
    return lax.dot_general(a, b, (((1,), (1,)), ((), ())), preferred_element_type=F32)


def _ffn_kernel(h_ref, g_ref, wg_ref, wu_ref, wd_ref, *rest, nf, dn, final):
    if final:
        fg_ref, o_ref, n_ref = rest
    else:
        o_ref, n_ref = rest
    f = pl.program_id(1)

    @pl.when(f == 0)
    def _():
        n_ref[...] = _rms(h_ref[...], g_ref[...]).astype(BF16)
        o_ref[...] = jnp.zeros(o_ref.shape, F32)

    n = n_ref[...]
    gate = _dot(n, wg_ref[...])
    up = _dot(n, wu_ref[...])
    a = (gate * (1.0 / (1.0 + jnp.exp(-gate))) * up).astype(BF16)
    d = o_ref.shape[1]
    for c in range(d // dn):
        sl = slice(c * dn, (c + 1) * dn)
        o_ref[:, sl] += _dot(a, wd_ref[:, sl])

    @pl.when(f == nf - 1)
    def _():
        y = h_ref[...] + FFN_HALF * o_ref[...]
        if final:
            y = _rms(y, fg_ref[...])
        o_ref[...] = y


def _ffn(h, gain, wg, wu, wd, final_gain=None, *, tm=512, tf=512, dn=512):
    m, d = h.shape
    dff = wg.shape[1]
    nf = dff // tf
    final = final_gain is not None
    in_specs = [
        pl.BlockSpec((tm, d), lambda i, f: (i, 0)),
        pl.BlockSpec((1, d), lambda i, f: (0, 0)),
        pl.BlockSpec((d, tf), lambda i, f: (0, f)),
        pl.BlockSpec((d, tf), lambda i, f: (0, f)),
        pl.BlockSpec((tf, d), lambda i, f: (f, 0)),
    ]
    args = [h, gain.reshape(1, d), wg, wu, wd]
    if final:
        in_specs.append(pl.BlockSpec((1, d), lambda i, f: (0, 0)))
        args.append(final_gain.reshape(1, d))
    return pl.pallas_call(
        functools.partial(_ffn_kernel, nf=nf, dn=dn, final=final),
        grid=(m // tm, nf),
        in_specs=in_specs,
        out_specs=pl.BlockSpec((tm, d), lambda i, f: (i, 0)),
        out_shape=jax.ShapeDtypeStruct((m, d), F32),
        scratch_shapes=[pltpu.VMEM((tm, d), BF16)],
        compiler_params=_cparams(("parallel", "arbitrary")),
        name="ffn_final" if final else "ffn",
    )(*args)


def _qkv_kernel(h_ref, g_ref, w_ref, o_ref, n_ref, *, q_blocks, scale):
    j = pl.program_id(1)

    @pl.when(j == 0)
    def _():
        n_ref[...] = _rms(h_ref[...], g_ref[...]).astype(BF16)

    acc = _dot(n_ref[...], w_ref[...])
    acc = acc * jnp.where(j < q_blocks, scale, 1.0).astype(F32)
    for c in range(o_ref.shape[0]):
        o_ref[c] = acc[:, c * HEAD_DIM:(c + 1) * HEAD_DIM].astype(BF16)


def _qkv(h, gain, w, *, tm=512, tn=1024):
    m, d = h.shape
    n3 = w.shape[1]
    hpb = tn // HEAD_DIM
    return pl.pallas_call(
        functools.partial(_qkv_kernel, q_blocks=(n3 // 3) // tn, scale=HEAD_DIM ** -0.5),
        grid=(m // tm, n3 // tn),
        in_specs=[
            pl.BlockSpec((tm, d), lambda i, j: (i, 0)),
            pl.BlockSpec((1, d), lambda i, j: (0, 0)),
            pl.BlockSpec((d, tn), lambda i, j: (0, j)),
        ],
        out_specs=pl.BlockSpec((hpb, tm, HEAD_DIM), lambda i, j: (j, i, 0)),
        out_shape=jax.ShapeDtypeStruct((n3 // HEAD_DIM, m, HEAD_DIM), BF16),
        scratch_shapes=[pltpu.VMEM((tm, d), BF16)],
        compiler_params=_cparams(("parallel", "arbitrary")),
        name="qkv",
    )(h, gain.reshape(1, d), w)


def _oproj_kernel(h_ref, *rest, n_in):
    a_refs = rest[:n_in]
    w_ref = rest[n_in]
    o_ref = rest[n_in + 1]
    acc = h_ref[...]
    off = 0
    for a_ref in a_refs:
        k = a_ref.shape[1]
        acc = acc + _dot(a_ref[...], w_ref[off:off + k, :])
        off += k
    o_ref[...] = acc


def _oproj(h, attn_parts, w, *, tm=512):
    m, d = h.shape
    in_specs = [pl.BlockSpec((tm, d), lambda i: (i, 0))]
    for a in attn_parts:
        in_specs.append(pl.BlockSpec((tm, a.shape[1]), lambda i: (i, 0)))
    in_specs.append(pl.BlockSpec(w.shape, lambda i: (0, 0)))
    return pl.pallas_call(
        functools.partial(_oproj_kernel, n_in=len(attn_parts)),
        grid=(m // tm,),
        in_specs=in_specs,
        out_specs=pl.BlockSpec((tm, d), lambda i: (i, 0)),
        out_shape=jax.ShapeDtypeStruct((m, d), F32),
        compiler_params=_cparams(("parallel",)),
        name="oproj",
    )(h, *attn_parts, w)


def _bucket_bias(dist, tab_ref, head):
    max_exact = REL_BUCKETS // 2
    d = jnp.maximum(dist, 0)
    df = jnp.maximum(d, 1).astype(F32)
    large = max_exact + (jnp.log(df / max_exact) / math.log(REL_MAX_DIST / max_exact)
                         * (REL_BUCKETS - max_exact)).astype(jnp.int32)
    large = jnp.minimum(large, REL_BUCKETS - 1)
    bucket = jnp.where(d < max_exact, d, large)
    out = jnp.full(dist.shape, tab_ref[0, head], F32)
    for b in range(1, REL_BUCKETS):
        out = jnp.where(bucket == b, tab_ref[b, head], out)
    return out


def _dilated_kernel(tab_ref, q_ref, k_ref, v_ref, o_ref, bias_ref, stage_ref, num_ref, lse_ref, *, seq, head0):
    nbr = len(DILATIONS)
    h = pl.program_id(0)
    b = pl.program_id(1)
    w = DIL_W
    j = lax.broadcasted_iota(jnp.int32, (w, 2 * w), 1)

    @pl.when(b == 0)
    def _():
        i = lax.broadcasted_iota(jnp.int32, (w, 2 * w), 0)
        rel = w + i - j
        ok = (rel >= 0) & (rel <= w)
        for br, r in enumerate(DILATIONS):
            bias_ref[br] = jnp.where(ok, _bucket_bias(rel * r, tab_ref, head0 + h), NEG)

    no_prev = jnp.where(j < w, NEG, 0.0).astype(F32)

    for t, ref in enumerate((q_ref, k_ref, v_ref)):
        stage_ref[t] = ref[0, 0].astype(F32)

    for br, r in enumerate(DILATIONS):
        sub = seq // r
        nb = sub // w
        for c in range(r):
            rows = pl.ds(c, sub, stride=r) if r > 1 else slice(None)
            if r > 1:
                qc, kc, vc = (stage_ref[t, rows, :].astype(BF16) for t in range(3))
            else:
                qc, kc, vc = q_ref[0, 0], k_ref[0, 0], v_ref[0, 0]

            def band(t):
                prev = jnp.concatenate([t[:w], t[:sub - w]], axis=0) if nb > 1 else t
                return jnp.concatenate([prev.reshape(nb, w, HEAD_DIM), t.reshape(nb, w, HEAD_DIM)], axis=1)

            s = lax.dot_general(qc.reshape(nb, w, HEAD_DIM), band(kc), (((2,), (2,)), ((0,), (0,))),
                                preferred_element_type=F32) + bias_ref[br][None]
            s = jnp.concatenate([s[:1] + no_prev[None], s[1:]], axis=0) if nb > 1 else s + no_prev[None]
            m = jnp.max(s, axis=-1, keepdims=True)
            p = jnp.exp(s - m)
            den = jnp.sum(p, axis=-1, keepdims=True)
            o = lax.dot_general(p.astype(BF16), band(vc), (((2,), (1,)), ((0,), (0,))),
                                preferred_element_type=F32) / den
            lse = m + jnp.log(den)
            num_ref[br, rows, :] = o.reshape(sub, HEAD_DIM)
            lse_ref[br, rows, :] = jnp.broadcast_to(lse, (nb, w, HEAD_DIM)).reshape(sub, HEAD_DIM)

    ch = 512

    def combine(t, carry):
        rows = pl.ds(pl.multiple_of(t * ch, ch), ch)
        ls = [lse_ref[br, rows, :] for br in range(nbr)]
        m = functools.reduce(jnp.maximum, ls)
        ws = [jnp.exp(l - m) for l in ls]
        tot = functools.reduce(lambda x, y: x + y, ws)
        acc = functools.reduce(lambda x, y: x + y, [wt * num_ref[br, rows, :] for br, wt in enumerate(ws)])
        o_ref[0, rows, :] = (acc / tot).astype(o_ref.dtype)
        return carry

    lax.fori_loop(0, seq // ch, combine, 0)


def _dilated(qkv, rel_bias, *, batch, seq, head0=0, n_heads=N_HEADS_A):
    nh_all = qkv.shape[0] // 3
    view = qkv.reshape(3 * nh_all, batch, seq, HEAD_DIM)
    nbr = len(DILATIONS)
    return pl.pallas_call(
        functools.partial(_dilated_kernel, seq=seq, head0=head0),
        grid=(n_heads, batch),
        in_specs=[pl.BlockSpec(memory_space=pltpu.SMEM)] + [
            pl.BlockSpec((1, 1, seq, HEAD_DIM), lambda h, b, j=j: (j * nh_all + head0 + h, b, 0, 0))
            for j in range(3)],
        out_specs=pl.BlockSpec((1, seq, HEAD_DIM), lambda h, b: (b, 0, h)),
        out_shape=jax.ShapeDtypeStruct((batch, seq, n_heads * HEAD_DIM), BF16),
        scratch_shapes=[
            pltpu.VMEM((nbr, DIL_W, 2 * DIL_W), F32),
            pltpu.VMEM((3, seq, HEAD_DIM), F32),
            pltpu.VMEM((nbr, seq, HEAD_DIM), F32),
            pltpu.VMEM((nbr, seq, HEAD_DIM), F32),
        ],
        compiler_params=_cparams(("arbitrary", "arbitrary")),
        name="dilated",
    )(rel_bias, view, view, view)


def _moba_kernel(tab_ref, q_ref, k_ref, v_ref, o_ref, bias_ref, qa_ref, ka_ref, *, seq, head0):
    blk = MOBA_BLOCK
    nblk = seq // blk
    hd = HEAD_DIM
    h = pl.program_id(0)
    b = pl.program_id(1)

    @pl.when(b == 0)
    def _():
        r = lax.broadcasted_iota(jnp.int32, (blk, blk), 0)
        c = lax.broadcasted_iota(jnp.int32, (blk, blk), 1)

        def fill(dlt, carry):
            dist = dlt * blk + r - c
            bias_ref[dlt] = jnp.where(dist >= 0, _bucket_bias(dist, tab_ref, head0 + h), NEG)
            return carry

        lax.fori_loop(0, nblk, fill, 0)

    @pl.when((h == 0) & (b == 0))
    def _():
        key_blk = lax.broadcasted_iota(jnp.int32, (seq, hd), 0) // blk
        lane = lax.broadcasted_iota(jnp.int32, (seq, hd), 1)
        ka_ref[:, hd:] = jnp.where(lane == key_blk, MOBA_MASK, 0.0).astype(BF16)

    q = q_ref[0, 0]
    k = k_ref[0, 0]
    qa_ref[:, :hd] = q
    ka_ref[:, :hd] = k

    kbar = jnp.sum(k.astype(F32).reshape(nblk, blk, hd), axis=1) * (1.0 / blk)
    k1 = kbar.astype(BF16)
    r1 = kbar - k1.astype(F32)
    k2 = r1.astype(BF16)
    k3 = (r1 - k2.astype(F32)).astype(BF16)
    gate = _dot_nt(k1, q) + _dot_nt(k2, q) + _dot_nt(k3, q)
    n_idx = lax.broadcasted_iota(jnp.int32, gate.shape, 0)
    own = lax.broadcasted_iota(jnp.int32, gate.shape, 1) // blk
    gate = jnp.where(n_idx < own, gate, -jnp.inf)
    skip = jnp.where(n_idx == own, 0.0, 1.0)
    for _ in range(min(MOBA_TOPK, nblk)):
        best = jnp.max(gate, axis=0, keepdims=True)
        idx = jnp.min(jnp.where(gate == best, n_idx, nblk), axis=0, keepdims=True)
        skip = jnp.where((n_idx == idx) & (best > -jnp.inf), 0.0, skip)
        gate = jnp.where(n_idx == idx, -jnp.inf, gate)
    skip = jnp.concatenate([skip, jnp.zeros((hd - nblk, seq), F32)], axis=0)
    qa_ref[:, hd:] = skip.T.astype(BF16)

    group = MOBA_GROUP
    for ng in range(1, nblk // group + 1):
        width = ng * group * blk

        def body(i, carry, ng=ng, width=width):
            rows = pl.ds(pl.multiple_of(i * blk, blk), blk)
            s = _dot_nt(qa_ref[rows, :], ka_ref[:width, :])
            s = s + jnp.concatenate([bias_ref[jnp.maximum(i - n, 0)] for n in range(ng * group)], axis=1)
            m = jnp.max(s, axis=-1, keepdims=True)
            p = jnp.exp(s - m)
            den = jnp.sum(p, axis=-1, keepdims=True)
            o = _dot(p.astype(BF16), v_ref[0, 0, :width, :]) / den
            o_ref[0, rows, :] = o.astype(o_ref.dtype)
            return carry

        lax.fori_loop((ng - 1) * group, ng * group, body, 0)


def _moba(qkv, rel_bias, *, batch, seq, head0=N_HEADS_A, n_heads=N_HEADS - N_HEADS_A):
    nh_all = qkv.shape[0] // 3
    blk = MOBA_BLOCK
    nblk = seq // blk
    assert nblk % MOBA_GROUP == 0 and nblk <= HEAD_DIM
    view = qkv.reshape(3 * nh_all, batch, seq, HEAD_DIM)
    return pl.pallas_call(
        functools.partial(_moba_kernel, seq=seq, head0=head0),
        grid=(n_heads, batch),
        in_specs=[pl.BlockSpec(memory_space=pltpu.SMEM)] + [
            pl.BlockSpec((1, 1, seq, HEAD_DIM), lambda h, b, j=j: (j * nh_all + head0 + h, b, 0, 0))
            for j in range(3)],
        out_specs=pl.BlockSpec((1, seq, HEAD_DIM), lambda h, b: (b, 0, h)),
        out_shape=jax.ShapeDtypeStruct((batch, seq, n_heads * HEAD_DIM), BF16),
        scratch_shapes=[
            pltpu.VMEM((nblk, blk, blk), F32),
            pltpu.VMEM((seq, 2 * HEAD_DIM), BF16),
            pltpu.VMEM((seq, 2 * HEAD_DIM), BF16),
        ],
        compiler_params=_cparams(("arbitrary", "arbitrary")),
        name="moba",
    )(rel_bias, view, view, view)


def _sb_kernel(q_ref, k_ref, v_ref, o_ref, c_ref, acc_ref):
    tq, tk = SB_TQ, SB_TK
    i = pl.program_id(2)
    q = q_ref[0, 0]
    rows = lax.broadcasted_iota(jnp.int32, (tq, tk), 0)
    cols = lax.broadcasted_iota(jnp.int32, (tq, tk), 1)
    uj = lax.broadcasted_iota(jnp.int32, (2 * tk, tk), 0)
    us = lax.broadcasted_iota(jnp.int32, (2 * tk, tk), 1)
    suffix = jnp.where((uj % tk) > us, 1.0, 0.0).astype(BF16)

    c_ref[...] = jnp.zeros(c_ref.shape, F32)
    acc_ref[...] = jnp.zeros(acc_ref.shape, F32)

    def block(j, masked):
        krows = pl.ds(pl.multiple_of(j * tk, tk), tk)
        z = _dot_nt(q, k_ref[0, 0, krows, :])
        sp = jnp.maximum(z, 0.0) + jnp.log(1.0 + jnp.exp(-jnp.abs(z)))
        lk = -sp
        if masked:
            past = (j * tk + cols) < (i * tq + rows)
            lk = jnp.where(past, lk, 0.0)
        hi = lk.astype(BF16)
        lo = (lk - hi.astype(F32)).astype(BF16)
        between = _dot(jnp.concatenate([hi, lo], axis=1), suffix) + c_ref[...]
        logw = (z - sp) + between
        if masked:
            logw = jnp.where(past, logw, -jnp.inf)
        p = jnp.exp(logw).astype(BF16)
        acc_ref[...] += _dot(p, v_ref[0, 0, krows, :])
        c_ref[...] += jnp.sum(lk, axis=-1, keepdims=True)

    ndiag = tq // tk
    for d in range(ndiag):
        block(i * ndiag + (ndiag - 1 - d), True)

    def cond(state):
        j, cmax = state
        return (j >= 0) & (cmax > SB_DEAD)

    def body(state):
        j, _ = state
        block(j, False)
        return j - 1, jnp.max(c_ref[...])

    lax.while_loop(cond, body, (i * ndiag - 1, jnp.max(c_ref[...])))
    o_ref[0] = acc_ref[...].astype(o_ref.dtype)


def _stickbreaking(qkv, *, batch, seq):
    nh = qkv.shape[0] // 3
    view = qkv.reshape(3 * nh, batch, seq, HEAD_DIM)
    return pl.pallas_call(
        _sb_kernel,
        grid=(batch, nh, seq // SB_TQ),
        in_specs=[
            pl.BlockSpec((1, 1, SB_TQ, HEAD_DIM), lambda b, h, i: (h, b, i, 0)),
            pl.BlockSpec((1, 1, seq, HEAD_DIM), lambda b, h, i: (nh + h, b, 0, 0)),
            pl.BlockSpec((1, 1, seq, HEAD_DIM), lambda b, h, i: (2 * nh + h, b, 0, 0)),
        ],
        out_specs=pl.BlockSpec((1, SB_TQ, HEAD_DIM), lambda b, h, i: (b, i, h)),
        out_shape=jax.ShapeDtypeStruct((batch, seq, nh * HEAD_DIM), BF16),
        scratch_shapes=[pltpu.VMEM((SB_TQ, 1), F32), pltpu.VMEM((SB_TQ, HEAD_DIM), F32)],
        compiler_params=_cparams(("parallel", "parallel", "arbitrary")),
        name="stickbreaking",
    )(view, view, view)


def kernel(x, ln_gains, ffn_w_gate, ffn_w_up, ffn_w_down, w_qkv_even, w_out_even, w_qkv_odd, w_out_odd,
           rel_bias, final_gain):
    batch, seq, d = x.shape
    depth = ln_gains.shape[0]
    m = batch * seq
    wg, wu, wd = (w.astype(BF16) for w in (ffn_w_gate, ffn_w_up, ffn_w_down))
    wqkv = (w_qkv_even.astype(BF16), w_qkv_odd.astype(BF16))
    wout = (w_out_even.astype(BF16), w_out_odd.astype(BF16))
    rel_bias = rel_bias.astype(F32)

    h = x.reshape(m, d)
    for i in range(depth):
        h = _ffn(h, ln_gains[i, 0], wg[i, 0], wu[i, 0], wd[i, 0])
        qkv = _qkv(h, ln_gains[i, 1], wqkv[i % 2][i // 2])
        if i % 2 == 0:
            parts = [_dilated(qkv, rel_bias, batch=batch, seq=seq), _moba(qkv, rel_bias, batch=batch, seq=seq)]
        else:
            parts = [_stickbreaking(qkv, batch=batch, seq=seq)]
        parts = [p.reshape(m, p.shape[-1]) for p in parts]
        h = _oproj(h, parts, wout[i % 2][i // 2])
        last = i == depth - 1
        h = _ffn(h, ln_gains[i, 2], wg[i, 1], wu[i, 1], wd[i, 1], final_gain if last else None)
    return h.reshape(batch, seq, d)
```

```python
import functools
import math

import jax
import jax.numpy as jnp
from jax import lax
from jax.experimental import pallas as pl
from jax.experimental.pallas import tpu as pltpu

F32 = jnp.float32
BF16 = jnp.bfloat16

HEAD_DIM = 128
N_HEADS = 16
N_HEADS_A = 8
DILATIONS = (1, 4, 16)
DIL_W = 128
MOBA_BLOCK = 256
MOBA_TOPK = 3
MOBA_GROUP = 4
MOBA_MASK = -(2.0 ** 100)
REL_BUCKETS = 32
REL_MAX_DIST = 2048
FFN_HALF = 0.5
RMS_EPS = 1e-6
NEG = -1e30
SB_T = 256
SB_HEADS = 4
SB_DEAD = -105.0
V7X_VMEM_LIMIT = 56 * 1024 * 1024


def _cparams(sem):
    return pltpu.CompilerParams(dimension_semantics=sem, vmem_limit_bytes=V7X_VMEM_LIMIT)


def _rms(x, g):
    ms = jnp.mean(x * x, axis=-1, keepdims=True)
    return x * lax.rsqrt(ms + RMS_EPS) * g


def _dot(a, b):
    return jnp.dot(a, b, preferred_element_type=F32)


def _dot_nt(a, b):
    return lax.dot_general(a, b, (((1,), (1,)), ((), ())), preferred_element_type=F32)


def _ffn_kernel(h_ref, g_ref, wg_ref, wu_ref, wd_ref, *rest, nf, dn, final):
    if final:
        fg_ref, o_ref, n_ref = rest
    else:
        o_ref, n_ref = rest
    f = pl.program_id(1)

    @pl.when(f == 0)
    def _():
        n_ref[...] = _rms(h_ref[...], g_ref[...]).astype(BF16)
        o_ref[...] = jnp.zeros(o_ref.shape, F32)

    n = n_ref[...]
    gate = _dot(n, wg_ref[...])
    up = _dot(n, wu_ref[...])
    a = (gate * (1.0 / (1.0 + jnp.exp(-gate))) * up).astype(BF16)
    d = o_ref.shape[1]
    for c in range(d // dn):
        sl = slice(c * dn, (c + 1) * dn)
        o_ref[:, sl] += _dot(a, wd_ref[:, sl])

    @pl.when(f == nf - 1)
    def _():
        y = h_ref[...] + FFN_HALF * o_ref[...]
        if final:
            y = _rms(y, fg_ref[...])
        o_ref[...] = y


def _ffn(h, gain, wg, wu, wd, final_gain=None, *, tm=512, tf=512, dn=512):
    m, d = h.shape
    dff = wg.shape[1]
    nf = dff // tf
    final = final_gain is not None
    in_specs = [
        pl.BlockSpec((tm, d), lambda i, f: (i, 0)),
        pl.BlockSpec((1, d), lambda i, f: (0, 0)),
        pl.BlockSpec((d, tf), lambda i, f: (0, f)),
        pl.BlockSpec((d, tf), lambda i, f: (0, f)),
        pl.BlockSpec((tf, d), lambda i, f: (f, 0)),
    ]
    args = [h, gain.reshape(1, d), wg, wu, wd]
    if final:
        in_specs.append(pl.BlockSpec((1, d), lambda i, f: (0, 0)))
        args.append(final_gain.reshape(1, d))
    return pl.pallas_call(
        functools.partial(_ffn_kernel, nf=nf, dn=dn, final=final),
        grid=(m // tm, nf),
        in_specs=in_specs,
        out_specs=pl.BlockSpec((tm, d), lambda i, f: (i, 0)),
        out_shape=jax.ShapeDtypeStruct((m, d), F32),
        scratch_shapes=[pltpu.VMEM((tm, d), BF16)],
        compiler_params=_cparams(("parallel", "arbitrary")),
        name="ffn_final" if final else "ffn",
    )(*args)


def _qkv_kernel(h_ref, g_ref, w_ref, o_ref, n_ref, *, q_blocks, scale):
    j = pl.program_id(1)

    @pl.when(j == 0)
    def _():
        n_ref[...] = _rms(h_ref[...], g_ref[...]).astype(BF16)

    acc = _dot(n_ref[...], w_ref[...])
    acc = acc * jnp.where(j < q_blocks, scale, 1.0).astype(F32)
    for c in range(o_ref.shape[0]):
        o_ref[c] = acc[:, c * HEAD_DIM:(c + 1) * HEAD_DIM].astype(BF16)


def _qkv(h, gain, w, *, tm=1024, tn=1024):
    m, d = h.shape
    n3 = w.shape[1]
    hpb = tn // HEAD_DIM
    return pl.pallas_call(
        functools.partial(_qkv_kernel, q_blocks=(n3 // 3) // tn, scale=HEAD_DIM ** -0.5),
        grid=(m // tm, n3 // tn),
        in_specs=[
            pl.BlockSpec((tm, d), lambda i, j: (i, 0)),
            pl.BlockSpec((1, d), lambda i, j: (0, 0)),
            pl.BlockSpec((d, tn), lambda i, j: (0, j)),
        ],
        out_specs=pl.BlockSpec((hpb, tm, HEAD_DIM), lambda i, j: (j, i, 0)),
        out_shape=jax.ShapeDtypeStruct((n3 // HEAD_DIM, m, HEAD_DIM), BF16),
        scratch_shapes=[pltpu.VMEM((tm, d), BF16)],
        compiler_params=_cparams(("parallel", "arbitrary")),
        name="qkv",
    )(h, gain.reshape(1, d), w)


def _oproj_kernel(h_ref, *rest, n_in):
    a_refs = rest[:n_in]
    w_ref = rest[n_in]
    o_ref = rest[n_in + 1]
    acc = h_ref[...]
    off = 0
    for a_ref in a_refs:
        k = a_ref.shape[1]
        acc = acc + _dot(a_ref[...], w_ref[off:off + k, :])
        off += k
    o_ref[...] = acc


def _oproj(h, attn_parts, w, *, tm=512):
    m, d = h.shape
    in_specs = [pl.BlockSpec((tm, d), lambda i: (i, 0))]
    for a in attn_parts:
        in_specs.append(pl.BlockSpec((tm, a.shape[1]), lambda i: (i, 0)))
    in_specs.append(pl.BlockSpec(w.shape, lambda i: (0, 0)))
    return pl.pallas_call(
        functools.partial(_oproj_kernel, n_in=len(attn_parts)),
        grid=(m // tm,),
        in_specs=in_specs,
        out_specs=pl.BlockSpec((tm, d), lambda i: (i, 0)),
        out_shape=jax.ShapeDtypeStruct((m, d), F32),
        compiler_params=_cparams(("parallel",)),
        name="oproj",
    )(h, *attn_parts, w)


def _bucket_bias(dist, tab_ref, head):
    max_exact = REL_BUCKETS // 2
    d = jnp.maximum(dist, 0)
    df = jnp.maximum(d, 1).astype(F32)
    large = max_exact + (jnp.log(df / max_exact) / math.log(REL_MAX_DIST / max_exact)
                         * (REL_BUCKETS - max_exact)).astype(jnp.int32)
    large = jnp.minimum(large, REL_BUCKETS - 1)
    bucket = jnp.where(d < max_exact, d, large)
    out = jnp.full(dist.shape, tab_ref[0, head], F32)
    for b in range(1, REL_BUCKETS):
        out = jnp.where(bucket == b, tab_ref[b, head], out)
    return out


def _dilated_kernel(tab_ref, q_ref, k_ref, v_ref, o_ref, bias_ref, stage_ref, num_ref, lse_ref, *, seq, head0):
    nbr = len(DILATIONS)
    h = pl.program_id(0)
    b = pl.program_id(1)
    w = DIL_W
    j = lax.broadcasted_iota(jnp.int32, (w, 2 * w), 1)

    @pl.when(b == 0)
    def _():
        i = lax.broadcasted_iota(jnp.int32, (w, 2 * w), 0)
        rel = w + i - j
        ok = (rel >= 0) & (rel <= w)
        for br, r in enumerate(DILATIONS):
            bias_ref[br] = jnp.where(ok, _bucket_bias(rel * r, tab_ref, head0 + h), NEG)

    no_prev = jnp.where(j < w, NEG, 0.0).astype(F32)

    for t, ref in enumerate((q_ref, k_ref, v_ref)):
        stage_ref[t] = ref[0, 0].astype(F32)

    for br, r in enumerate(DILATIONS):
        sub = seq // r
        nb = sub // w
        for c in range(r):
            rows = pl.ds(c, sub, stride=r) if r > 1 else slice(None)
            if r > 1:
                qc, kc, vc = (stage_ref[t, rows, :].astype(BF16) for t in range(3))
            else:
                qc, kc, vc = q_ref[0, 0], k_ref[0, 0], v_ref[0, 0]

            def band(t):
                prev = jnp.concatenate([t[:w], t[:sub - w]], axis=0) if nb > 1 else t
                return jnp.concatenate([prev.reshape(nb, w, HEAD_DIM), t.reshape(nb, w, HEAD_DIM)], axis=1)

            s = lax.dot_general(qc.reshape(nb, w, HEAD_DIM), band(kc), (((2,), (2,)), ((0,), (0,))),
                                preferred_element_type=F32) + bias_ref[br][None]
            s = jnp.concatenate([s[:1] + no_prev[None], s[1:]], axis=0) if nb > 1 else s + no_prev[None]
            m = jnp.max(s, axis=-1, keepdims=True)
            p = jnp.exp(s - m)
            den = jnp.sum(p, axis=-1, keepdims=True)
            o = lax.dot_general(p.astype(BF16), band(vc), (((2,), (1,)), ((0,), (0,))),
                                preferred_element_type=F32) / den
            lse = m + jnp.log(den)
            num_ref[br, rows, :] = o.reshape(sub, HEAD_DIM)
            lse_ref[br, rows, :] = jnp.broadcast_to(lse, (nb, w, HEAD_DIM)).reshape(sub, HEAD_DIM)

    ch = 512

    def combine(t, carry):
        rows = pl.ds(pl.multiple_of(t * ch, ch), ch)
        ls = [lse_ref[br, rows, :] for br in range(nbr)]
        m = functools.reduce(jnp.maximum, ls)
        ws = [jnp.exp(l - m) for l in ls]
        tot = functools.reduce(lambda x, y: x + y, ws)
        acc = functools.reduce(lambda x, y: x + y, [wt * num_ref[br, rows, :] for br, wt in enumerate(ws)])
        o_ref[0, rows, :] = (acc / tot).astype(o_ref.dtype)
        return carry

    lax.fori_loop(0, seq // ch, combine, 0)


def _dilated(qkv, rel_bias, *, batch, seq, head0=0, n_heads=N_HEADS_A):
    nh_all = qkv.shape[0] // 3
    view = qkv.reshape(3 * nh_all, batch, seq, HEAD_DIM)
    nbr = len(DILATIONS)
    return pl.pallas_call(
        functools.partial(_dilated_kernel, seq=seq, head0=head0),
        grid=(n_heads, batch),
        in_specs=[pl.BlockSpec(memory_space=pltpu.SMEM)] + [
            pl.BlockSpec((1, 1, seq, HEAD_DIM), lambda h, b, j=j: (j * nh_all + head0 + h, b, 0, 0))
            for j in range(3)],
        out_specs=pl.BlockSpec((1, seq, HEAD_DIM), lambda h, b: (b, 0, h)),
        out_shape=jax.ShapeDtypeStruct((batch, seq, n_heads * HEAD_DIM), BF16),
        scratch_shapes=[
            pltpu.VMEM((nbr, DIL_W, 2 * DIL_W), F32),
            pltpu.VMEM((3, seq, HEAD_DIM), F32),
            pltpu.VMEM((nbr, seq, HEAD_DIM), F32),
            pltpu.VMEM((nbr, seq, HEAD_DIM), F32),
        ],
        compiler_params=_cparams(("arbitrary", "arbitrary")),
        name="dilated",
    )(rel_bias, view, view, view)


def _moba_kernel(tab_ref, q_ref, k_ref, v_ref, o_ref, bias_ref, qa_ref, ka_ref, *, seq, head0):
    blk = MOBA_BLOCK
    nblk = seq // blk
    hd = HEAD_DIM
    h = pl.program_id(0)
    b = pl.program_id(1)

    @pl.when(b == 0)
    def _():
        r = lax.broadcasted_iota(jnp.int32, (blk, blk), 0)
        c = lax.broadcasted_iota(jnp.int32, (blk, blk), 1)

        def fill(dlt, carry):
            dist = dlt * blk + r - c
            bias_ref[dlt] = jnp.where(dist >= 0, _bucket_bias(dist, tab_ref, head0 + h), NEG)
            return carry

        lax.fori_loop(0, nblk, fill, 0)

    @pl.when((h == 0) & (b == 0))
    def _():
        key_blk = lax.broadcasted_iota(jnp.int32, (seq, hd), 0) // blk
        lane = lax.broadcasted_iota(jnp.int32, (seq, hd), 1)
        ka_ref[:, hd:] = jnp.where(lane == key_blk, MOBA_MASK, 0.0).astype(BF16)

    q = q_ref[0, 0]
    k = k_ref[0, 0]
    qa_ref[:, :hd] = q
    ka_ref[:, :hd] = k

    kbar = jnp.sum(k.astype(F32).reshape(nblk, blk, hd), axis=1) * (1.0 / blk)
    k1 = kbar.astype(BF16)
    r1 = kbar - k1.astype(F32)
    k2 = r1.astype(BF16)
    k3 = (r1 - k2.astype(F32)).astype(BF16)
    gate = _dot_nt(k1, q) + _dot_nt(k2, q) + _dot_nt(k3, q)
    n_idx = lax.broadcasted_iota(jnp.int32, gate.shape, 0)
    own = lax.broadcasted_iota(jnp.int32, gate.shape, 1) // blk
    gate = jnp.where(n_idx < own, gate, -jnp.inf)
    skip = jnp.where(n_idx == own, 0.0, 1.0)
    for _ in range(min(MOBA_TOPK, nblk)):
        best = jnp.max(gate, axis=0, keepdims=True)
        idx = jnp.min(jnp.where(gate == best, n_idx, nblk), axis=0, keepdims=True)
        skip = jnp.where((n_idx == idx) & (best > -jnp.inf), 0.0, skip)
        gate = jnp.where(n_idx == idx, -jnp.inf, gate)
    skip = jnp.concatenate([skip, jnp.zeros((hd - nblk, seq), F32)], axis=0)
    qa_ref[:, hd:] = skip.T.astype(BF16)

    group = MOBA_GROUP
    for ng in range(1, nblk // group + 1):
        width = ng * group * blk

        def body(i, carry, ng=ng, width=width):
            rows = pl.ds(pl.multiple_of(i * blk, blk), blk)
            s = _dot_nt(qa_ref[rows, :], ka_ref[:width, :])
            s = s + jnp.concatenate([bias_ref[jnp.maximum(i - n, 0)] for n in range(ng * group)], axis=1)
            m = jnp.max(s, axis=-1, keepdims=True)
            p = jnp.exp(s - m)
            den = jnp.sum(p, axis=-1, keepdims=True)
            o = _dot(p.astype(BF16), v_ref[0, 0, :width, :]) / den
            o_ref[0, rows, :] = o.astype(o_ref.dtype)
            return carry

        lax.fori_loop((ng - 1) * group, ng * group, body, 0, unroll=2)


def _moba(qkv, rel_bias, *, batch, seq, head0=N_HEADS_A, n_heads=N_HEADS - N_HEADS_A):
    nh_all = qkv.shape[0] // 3
    blk = MOBA_BLOCK
    nblk = seq // blk
    assert nblk % MOBA_GROUP == 0 and nblk <= HEAD_DIM
    view = qkv.reshape(3 * nh_all, batch, seq, HEAD_DIM)
    return pl.pallas_call(
        functools.partial(_moba_kernel, seq=seq, head0=head0),
        grid=(n_heads, batch),
        in_specs=[pl.BlockSpec(memory_space=pltpu.SMEM)] + [
            pl.BlockSpec((1, 1, seq, HEAD_DIM), lambda h, b, j=j: (j * nh_all + head0 + h, b, 0, 0))
            for j in range(3)],
        out_specs=pl.BlockSpec((1, seq, HEAD_DIM), lambda h, b: (b, 0, h)),
        out_shape=jax.ShapeDtypeStruct((batch, seq, n_heads * HEAD_DIM), BF16),
        scratch_shapes=[
            pltpu.VMEM((nblk, blk, blk), F32),
            pltpu.VMEM((seq, 2 * HEAD_DIM), BF16),
            pltpu.VMEM((seq, 2 * HEAD_DIM), BF16),
        ],
        compiler_params=_cparams(("arbitrary", "arbitrary")),
        name="moba",
    )(rel_bias, view, view, view)


def _sb_kernel(q_ref, k_ref, v_ref, o_ref, c_ref, acc_ref):
    t = SB_T
    i = pl.program_id(2)
    rows = lax.broadcasted_iota(jnp.int32, (t, t), 0)
    cols = lax.broadcasted_iota(jnp.int32, (t, t), 1)
    causal = cols < rows
    suffix = jnp.where(rows > cols, 1.0, 0.0).astype(BF16)

    def tile(q, krows, hh, keep):
        z = _dot_nt(q, k_ref[hh, 0, krows, :])
        sp = jnp.maximum(z, 0.0) + jnp.log(1.0 + jnp.exp(-jnp.abs(z)))
        lk = -sp
        if keep is not None:
            lk = jnp.where(keep, lk, 0.0)
        logw = (z - sp) + _dot(lk.astype(BF16), suffix)
        if keep is not None:
            logw = jnp.where(keep, logw, -jnp.inf)
        return lk, logw

    has_prev = i > 0
    own_rows = pl.ds(pl.multiple_of(i * t, t), t)
    prev_rows = pl.ds(pl.multiple_of(jnp.maximum(i - 1, 0) * t, t), t)
    for hh in range(SB_HEADS):
        q = q_ref[hh, 0]
        lk_a, lw_a = tile(q, own_rows, hh, causal)
        lk_b, lw_b = tile(q, prev_rows, hh, None)
        c_a = jnp.sum(lk_a, axis=-1, keepdims=True)
        p_a = jnp.exp(lw_a).astype(BF16)
        p_b = jnp.exp(jnp.where(has_prev, lw_b + c_a, -jnp.inf)).astype(BF16)
        acc_ref[hh] = _dot(p_a, v_ref[hh, 0, own_rows, :]) + _dot(p_b, v_ref[hh, 0, prev_rows, :])
        c_ref[hh] = c_a + jnp.where(has_prev, jnp.sum(lk_b, axis=-1, keepdims=True), 0.0)

    def cond(state):
        j, cmax = state
        return (j >= 0) & (cmax > SB_DEAD)

    def body(state):
        j, _ = state
        krows = pl.ds(pl.multiple_of(j * t, t), t)
        for hh in range(SB_HEADS):
            lk, lw = tile(q_ref[hh, 0], krows, hh, None)
            acc_ref[hh] += _dot(jnp.exp(lw + c_ref[hh]).astype(BF16), v_ref[hh, 0, krows, :])
            c_ref[hh] += jnp.sum(lk, axis=-1, keepdims=True)
        return j - 1, jnp.max(c_ref[...])

    lax.while_loop(cond, body, (i - 2, jnp.max(c_ref[...])))
    o_ref[0] = jnp.concatenate([acc_ref[hh] for hh in range(SB_HEADS)], axis=1).astype(o_ref.dtype)


def _stickbreaking(qkv, *, batch, seq):
    nh = qkv.shape[0] // 3
    view = qkv.reshape(3 * nh, batch, seq, HEAD_DIM)
    g = SB_HEADS
    assert nh % g == 0
    return pl.pallas_call(
        _sb_kernel,
        grid=(batch, nh // g, seq // SB_T),
        in_specs=[
            pl.BlockSpec((g, 1, SB_T, HEAD_DIM), lambda b, h, i: (h, b, i, 0)),
            pl.BlockSpec((g, 1, seq, HEAD_DIM), lambda b, h, i: (nh // g + h, b, 0, 0)),
            pl.BlockSpec((g, 1, seq, HEAD_DIM), lambda b, h, i: (2 * (nh // g) + h, b, 0, 0)),
        ],
        out_specs=pl.BlockSpec((1, SB_T, g * HEAD_DIM), lambda b, h, i: (b, i, h)),
        out_shape=jax.ShapeDtypeStruct((batch, seq, nh * HEAD_DIM), BF16),
        scratch_shapes=[pltpu.VMEM((g, SB_T, 1), F32), pltpu.VMEM((g, SB_T, HEAD_DIM), F32)],
        compiler_params=_cparams(("parallel", "parallel", "arbitrary")),
        name="stickbreaking",
    )(view, view, view)


def kernel(x, ln_gains, ffn_w_gate, ffn_w_up, ffn_w_down, w_qkv_even, w_out_even, w_qkv_odd, w_out_odd,
           rel_bias, final_gain):
    batch, seq, d = x.shape
    depth = ln_gains.shape[0]
    m = batch * seq
    wg, wu, wd = (w.astype(BF16) for w in (ffn_w_gate, ffn_w_up, ffn_w_down))
    wqkv = (w_qkv_even.astype(BF16), w_qkv_odd.astype(BF16))
    wout = (w_out_even.astype(BF16), w_out_odd.astype(BF16))
    rel_bias = rel_bias.astype(F32)

    h = x.reshape(m, d)
    for i in range(depth):
        h = _ffn(h, ln_gains[i, 0], wg[i, 0], wu[i, 0], wd[i, 0])
        qkv = _qkv(h, ln_gains[i, 1], wqkv[i % 2][i // 2])
        if i % 2 == 0:
            parts = [_dilated(qkv, rel_bias, batch=batch, seq=seq), _moba(qkv, rel_bias, batch=batch, seq=seq)]
        else:
            parts = [_stickbreaking(qkv, batch=batch, seq=seq)]
        parts = [p.reshape(m, p.shape[-1]) for p in parts]
        h = _oproj(h, parts, wout[i % 2][i // 2])
        last = i == depth - 1
        h = _ffn(h, ln_gains[i, 2], wg[i, 1], wu[i, 1], wd[i, 1], final_gain if last else None)
    return h.reshape(batch, seq, d)
```

```python
import functools
import math

import jax
import jax.numpy as jnp
from jax import lax
from jax.experimental import pallas as pl
from jax.experimental.pallas import tpu as pltpu

F32 = jnp.float32
BF16 = jnp.bfloat16

HEAD_DIM = 128
N_HEADS = 16
N_HEADS_A = 8
DILATIONS = (1, 4, 16)
DIL_W = 128
MOBA_BLOCK = 256
MOBA_TOPK = 3
MOBA_GROUP = 2
LOG2E = math.log2(math.e)
MOBA_MASK = -(2.0 ** 100)
REL_BUCKETS = 32
REL_MAX_DIST = 2048
FFN_HALF = 0.5
RMS_EPS = 1e-6
NEG = -1e30
SB_T = 256
SB_HEADS = 4
SB_DEAD = 152.0
V7X_VMEM_LIMIT = 56 * 1024 * 1024


def _cparams(sem):
    return pltpu.CompilerParams(dimension_semantics=sem, vmem_limit_bytes=V7X_VMEM_LIMIT)


def _rms(x, g):
    ms = jnp.mean(x * x, axis=-1, keepdims=True)
    return x * lax.rsqrt(ms + RMS_EPS) * g


def _dot(a, b):
    return jnp.dot(a, b, preferred_element_type=F32)


def _dot_nt(a, b):
    return lax.dot_general(a, b, (((1,), (1,)), ((), ())), preferred_element_type=F32)


def _ffn_kernel(h_ref, g_ref, wg_ref, wu_ref, wd_ref, *rest, nf, dn, final):
    if final:
        fg_ref, o_ref, n_ref = rest
    else:
        o_ref, n_ref = rest
    f = pl.program_id(1)

    @pl.when(f == 0)
    def _():
        n_ref[...] = _rms(h_ref[...], g_ref[...]).astype(BF16)
        o_ref[...] = jnp.zeros(o_ref.shape, F32)

    n = n_ref[...]
    gate = _dot(n, wg_ref[...])
    up = _dot(n, wu_ref[...])
    a = (gate * (1.0 / (1.0 + jnp.exp(-gate))) * up).astype(BF16)
    d = o_ref.shape[1]
    for c in range(d // dn):
        sl = slice(c * dn, (c + 1) * dn)
        o_ref[:, sl] += _dot(a, wd_ref[:, sl])

    @pl.when(f == nf - 1)
    def _():
        y = h_ref[...] + FFN_HALF * o_ref[...]
        if final:
            y = _rms(y, fg_ref[...])
        o_ref[...] = y


def _ffn(h, gain, wg, wu, wd, layer, half, final_gain=None, *, tm=512, tf=512, dn=512):
    m, d = h.shape
    dff = wg.shape[-1]
    nf = dff // tf
    final = final_gain is not None
    in_specs = [
        pl.BlockSpec((tm, d), lambda i, f: (i, 0)),
        pl.BlockSpec((1, d), lambda i, f: (0, 0)),
        pl.BlockSpec((None, None, d, tf), lambda i, f: (layer, half, 0, f)),
        pl.BlockSpec((None, None, d, tf), lambda i, f: (layer, half, 0, f)),
        pl.BlockSpec((None, None, tf, d), lambda i, f: (layer, half, f, 0)),
    ]
    args = [h, gain.reshape(1, d), wg, wu, wd]
    if final:
        in_specs.append(pl.BlockSpec((1, d), lambda i, f: (0, 0)))
        args.append(final_gain.reshape(1, d))
    return pl.pallas_call(
        functools.partial(_ffn_kernel, nf=nf, dn=dn, final=final),
        grid=(m // tm, nf),
        in_specs=in_specs,
        out_specs=pl.BlockSpec((tm, d), lambda i, f: (i, 0)),
        out_shape=jax.ShapeDtypeStruct((m, d), F32),
        scratch_shapes=[pltpu.VMEM((tm, d), BF16)],
        compiler_params=_cparams(("parallel", "arbitrary")),
        name="ffn_final" if final else "ffn",
    )(*args)


def _qkv_kernel(h_ref, g_ref, w_ref, o_ref, n_ref, *, q_blocks, scale):
    j = pl.program_id(1)

    @pl.when(j == 0)
    def _():
        n_ref[...] = _rms(h_ref[...], g_ref[...]).astype(BF16)

    acc = _dot(n_ref[...], w_ref[...])
    acc = acc * jnp.where(j < q_blocks, scale, 1.0).astype(F32)
    for c in range(o_ref.shape[0]):
        o_ref[c] = acc[:, c * HEAD_DIM:(c + 1) * HEAD_DIM].astype(BF16)


def _qkv(h, gain, w, *, tm=1024, tn=1024):
    m, d = h.shape
    n3 = w.shape[1]
    hpb = tn // HEAD_DIM
    return pl.pallas_call(
        functools.partial(_qkv_kernel, q_blocks=(n3 // 3) // tn, scale=HEAD_DIM ** -0.5 * LOG2E),
        grid=(m // tm, n3 // tn),
        in_specs=[
            pl.BlockSpec((tm, d), lambda i, j: (i, 0)),
            pl.BlockSpec((1, d), lambda i, j: (0, 0)),
            pl.BlockSpec((d, tn), lambda i, j: (0, j)),
        ],
        out_specs=pl.BlockSpec((hpb, tm, HEAD_DIM), lambda i, j: (j, i, 0)),
        out_shape=jax.ShapeDtypeStruct((n3 // HEAD_DIM, m, HEAD_DIM), BF16),
        scratch_shapes=[pltpu.VMEM((tm, d), BF16)],
        compiler_params=_cparams(("parallel", "arbitrary")),
        name="qkv",
    )(h, gain.reshape(1, d), w)


def _oproj_kernel(h_ref, *rest, n_in):
    a_refs = rest[:n_in]
    w_ref = rest[n_in]
    o_ref = rest[n_in + 1]
    acc = h_ref[...]
    off = 0
    for a_ref in a_refs:
        k = a_ref.shape[1]
        acc = acc + _dot(a_ref[...], w_ref[off:off + k, :])
        off += k
    o_ref[...] = acc


def _oproj(h, attn_parts, w, *, tm=512):
    m, d = h.shape
    in_specs = [pl.BlockSpec((tm, d), lambda i: (i, 0))]
    for a in attn_parts:
        in_specs.append(pl.BlockSpec((tm, a.shape[1]), lambda i: (i, 0)))
    in_specs.append(pl.BlockSpec(w.shape, lambda i: (0, 0)))
    return pl.pallas_call(
        functools.partial(_oproj_kernel, n_in=len(attn_parts)),
        grid=(m // tm,),
        in_specs=in_specs,
        out_specs=pl.BlockSpec((tm, d), lambda i: (i, 0)),
        out_shape=jax.ShapeDtypeStruct((m, d), F32),
        compiler_params=_cparams(("parallel",)),
        name="oproj",
    )(h, *attn_parts, w)


def _bucket_bias(dist, tab_ref, head):
    max_exact = REL_BUCKETS // 2
    d = jnp.maximum(dist, 0)
    df = jnp.maximum(d, 1).astype(F32)
    large = max_exact + (jnp.log(df / max_exact) / math.log(REL_MAX_DIST / max_exact)
                         * (REL_BUCKETS - max_exact)).astype(jnp.int32)
    large = jnp.minimum(large, REL_BUCKETS - 1)
    bucket = jnp.where(d < max_exact, d, large)
    out = jnp.full(dist.shape, tab_ref[0, head], F32)
    for b in range(1, REL_BUCKETS):
        out = jnp.where(bucket == b, tab_ref[b, head], out)
    return out


def _dilated_kernel(tab_ref, q_ref, k_ref, v_ref, o_ref, bias_ref, stage_ref, num_ref, lse_ref, *, seq, head0):
    nbr = len(DILATIONS)
    h = pl.program_id(0)
    b = pl.program_id(1)
    w = DIL_W
    j = lax.broadcasted_iota(jnp.int32, (w, 2 * w), 1)

    @pl.when(b == 0)
    def _():
        i = lax.broadcasted_iota(jnp.int32, (w, 2 * w), 0)
        rel = w + i - j
        ok = (rel >= 0) & (rel <= w)
        for br, r in enumerate(DILATIONS):
            bias_ref[br] = jnp.where(ok, _bucket_bias(rel * r, tab_ref, head0 + h) * LOG2E, NEG)

    no_prev = jnp.where(j < w, NEG, 0.0).astype(F32)

    for t, ref in enumerate((q_ref, k_ref, v_ref)):
        stage_ref[t] = ref[0, 0].astype(F32)

    for br, r in enumerate(DILATIONS):
        sub = seq // r
        nb = sub // w
        for c in range(r):
            rows = pl.ds(c, sub, stride=r) if r > 1 else slice(None)
            if r > 1:
                qc, kc, vc = (stage_ref[t, rows, :].astype(BF16) for t in range(3))
            else:
                qc, kc, vc = q_ref[0, 0], k_ref[0, 0], v_ref[0, 0]

            def band(t):
                prev = jnp.concatenate([t[:w], t[:sub - w]], axis=0) if nb > 1 else t
                return jnp.concatenate([prev.reshape(nb, w, HEAD_DIM), t.reshape(nb, w, HEAD_DIM)], axis=1)

            s = lax.dot_general(qc.reshape(nb, w, HEAD_DIM), band(kc), (((2,), (2,)), ((0,), (0,))),
                                preferred_element_type=F32) + bias_ref[br][None]
            s = jnp.concatenate([s[:1] + no_prev[None], s[1:]], axis=0) if nb > 1 else s + no_prev[None]
            m = jnp.max(s, axis=-1, keepdims=True)
            p = jnp.exp2(s - m)
            den = jnp.sum(p, axis=-1, keepdims=True)
            o = lax.dot_general(p.astype(BF16), band(vc), (((2,), (1,)), ((0,), (0,))),
                                preferred_element_type=F32) / den
            lse = m + jnp.log(den) * LOG2E
            num_ref[br, rows, :] = o.reshape(sub, HEAD_DIM)
            lse_ref[br, rows, :] = jnp.broadcast_to(lse, (nb, w, HEAD_DIM)).reshape(sub, HEAD_DIM)

    ch = 512

    def combine(t, carry):
        rows = pl.ds(pl.multiple_of(t * ch, ch), ch)
        ls = [lse_ref[br, rows, :] for br in range(nbr)]
        m = functools.reduce(jnp.maximum, ls)
        ws = [jnp.exp2(l - m) for l in ls]
        tot = functools.reduce(lambda x, y: x + y, ws)
        acc = functools.reduce(lambda x, y: x + y, [wt * num_ref[br, rows, :] for br, wt in enumerate(ws)])
        o_ref[0, rows, :] = (acc / tot).astype(o_ref.dtype)
        return carry

    lax.fori_loop(0, seq // ch, combine, 0)


def _dilated(qkv, rel_bias, *, batch, seq, head0=0, n_heads=N_HEADS_A):
    nh_all = qkv.shape[0] // 3
    view = qkv.reshape(3 * nh_all, batch, seq, HEAD_DIM)
    nbr = len(DILATIONS)
    return pl.pallas_call(
        functools.partial(_dilated_kernel, seq=seq, head0=head0),
        grid=(n_heads, batch),
        in_specs=[pl.BlockSpec(memory_space=pltpu.SMEM)] + [
            pl.BlockSpec((1, 1, seq, HEAD_DIM), lambda h, b, j=j: (j * nh_all + head0 + h, b, 0, 0))
            for j in range(3)],
        out_specs=pl.BlockSpec((1, seq, HEAD_DIM), lambda h, b: (b, 0, h)),
        out_shape=jax.ShapeDtypeStruct((batch, seq, n_heads * HEAD_DIM), BF16),
        scratch_shapes=[
            pltpu.VMEM((nbr, DIL_W, 2 * DIL_W), F32),
            pltpu.VMEM((3, seq, HEAD_DIM), F32),
            pltpu.VMEM((nbr, seq, HEAD_DIM), F32),
            pltpu.VMEM((nbr, seq, HEAD_DIM), F32),
        ],
        compiler_params=_cparams(("arbitrary", "arbitrary")),
        name="dilated",
    )(rel_bias, view, view, view)


def _moba_kernel(tab_ref, q_ref, k_ref, v_ref, o_ref, bias_ref, qa_ref, ka_ref, va_ref, *, seq, head0):
    blk = MOBA_BLOCK
    nblk = seq // blk
    hd = HEAD_DIM
    h = pl.program_id(0)
    b = pl.program_id(1)

    @pl.when(b == 0)
    def _():
        r = lax.broadcasted_iota(jnp.int32, (blk, blk), 0)
        c = lax.broadcasted_iota(jnp.int32, (blk, blk), 1)

        def fill(dlt, carry):
            dist = dlt * blk + r - c
            bias_ref[dlt] = jnp.where(dist >= 0, _bucket_bias(dist, tab_ref, head0 + h) * LOG2E, NEG)
            return carry

        lax.fori_loop(0, nblk, fill, 0)

    @pl.when((h == 0) & (b == 0))
    def _():
        key_blk = lax.broadcasted_iota(jnp.int32, (seq, hd), 0) // blk
        lane = lax.broadcasted_iota(jnp.int32, (seq, hd), 1)
        ka_ref[:, hd:] = jnp.where(lane == key_blk, MOBA_MASK, 0.0).astype(BF16)
        va_ref[:, hd:] = jnp.where(lane == 0, 1.0, 0.0).astype(BF16)

    q = q_ref[0, 0]
    k = k_ref[0, 0]
    qa_ref[:, :hd] = q
    ka_ref[:, :hd] = k
    va_ref[:, :hd] = v_ref[0, 0]

    kbar = jnp.sum(k.astype(F32).reshape(nblk, blk, hd), axis=1) * (1.0 / blk)
    k1 = kbar.astype(BF16)
    r1 = kbar - k1.astype(F32)
    k2 = r1.astype(BF16)
    k3 = (r1 - k2.astype(F32)).astype(BF16)
    gate = _dot_nt(k1, q) + _dot_nt(k2, q) + _dot_nt(k3, q)
    n_idx = lax.broadcasted_iota(jnp.int32, gate.shape, 0)
    own = lax.broadcasted_iota(jnp.int32, gate.shape, 1) // blk
    gate = jnp.where(n_idx < own, gate, -jnp.inf)
    skip = jnp.where(n_idx == own, 0.0, 1.0)
    for _ in range(min(MOBA_TOPK, nblk)):
        best = jnp.max(gate, axis=0, keepdims=True)
        idx = jnp.min(jnp.where(gate == best, n_idx, nblk), axis=0, keepdims=True)
        skip = jnp.where((n_idx == idx) & (best > -jnp.inf), 0.0, skip)
        gate = jnp.where(n_idx == idx, -jnp.inf, gate)
    skip = jnp.concatenate([skip, jnp.zeros((hd - nblk, seq), F32)], axis=0)
    qa_ref[:, hd:] = skip.T.astype(BF16)

    group = MOBA_GROUP
    for ng in range(1, nblk // group + 1):
        width = ng * group * blk

        def body(i, carry, ng=ng, width=width):
            rows = pl.ds(pl.multiple_of(i * blk, blk), blk)
            s = _dot_nt(qa_ref[rows, :], ka_ref[:width, :])
            s = s + jnp.concatenate([bias_ref[jnp.maximum(i - n, 0)] for n in range(ng * group)], axis=1)
            m = jnp.max(s, axis=-1, keepdims=True)
            p = jnp.exp2(s - m).astype(BF16)
            o = _dot(p, va_ref[:width, :])
            o_ref[0, rows, :] = (o[:, :hd] / o[:, hd:hd + 1]).astype(o_ref.dtype)
            return carry

        lax.fori_loop((ng - 1) * group, ng * group, body, 0, unroll=2)


def _moba(qkv, rel_bias, *, batch, seq, head0=N_HEADS_A, n_heads=N_HEADS - N_HEADS_A):
    nh_all = qkv.shape[0] // 3
    blk = MOBA_BLOCK
    nblk = seq // blk
    assert nblk % MOBA_GROUP == 0 and nblk <= HEAD_DIM
    view = qkv.reshape(3 * nh_all, batch, seq, HEAD_DIM)
    return pl.pallas_call(
        functools.partial(_moba_kernel, seq=seq, head0=head0),
        grid=(n_heads, batch),
        in_specs=[pl.BlockSpec(memory_space=pltpu.SMEM)] + [
            pl.BlockSpec((1, 1, seq, HEAD_DIM), lambda h, b, j=j: (j * nh_all + head0 + h, b, 0, 0))
            for j in range(3)],
        out_specs=pl.BlockSpec((1, seq, HEAD_DIM), lambda h, b: (b, 0, h)),
        out_shape=jax.ShapeDtypeStruct((batch, seq, n_heads * HEAD_DIM), BF16),
        scratch_shapes=[
            pltpu.VMEM((nblk, blk, blk), F32),
            pltpu.VMEM((seq, 2 * HEAD_DIM), BF16),
            pltpu.VMEM((seq, 2 * HEAD_DIM), BF16),
            pltpu.VMEM((seq, 2 * HEAD_DIM), BF16),
        ],
        compiler_params=_cparams(("arbitrary", "arbitrary")),
        name="moba",
    )(rel_bias, view, view, view)


def _sb_kernel(q_ref, k_ref, v_ref, o_ref, c_ref, acc_ref):
    t = SB_T
    i = pl.program_id(2)
    rows = lax.broadcasted_iota(jnp.int32, (t, t), 0)
    cols = lax.broadcasted_iota(jnp.int32, (t, t), 1)
    causal = cols < rows
    suffix = jnp.where(rows > cols, 1.0, 0.0).astype(BF16)
    sign_bit = jnp.uint32(0x80000000)

    def tile(q, krows, hh, keep):
        z = _dot_nt(q, k_ref[hh, 0, krows, :])
        neg_abs = lax.bitcast_convert_type(lax.bitcast_convert_type(z, jnp.uint32) | sign_bit, F32)
        sp = jnp.maximum(z, 0.0) + jnp.log(1.0 + jnp.exp2(neg_abs)) * LOG2E
        nk = sp if keep is None else jnp.where(keep, sp, 0.0)
        logw = (z - sp) - _dot(nk.astype(BF16), suffix)
        if keep is not None:
            logw = jnp.where(keep, logw, -jnp.inf)
        return nk, logw

    has_prev = i > 0
    own_rows = pl.ds(pl.multiple_of(i * t, t), t)
    prev_rows = pl.ds(pl.multiple_of(jnp.maximum(i - 1, 0) * t, t), t)
    for hh in range(SB_HEADS):
        q = q_ref[hh, 0]
        nk_a, lw_a = tile(q, own_rows, hh, causal)
        nk_b, lw_b = tile(q, prev_rows, hh, None)
        c_a = jnp.sum(nk_a, axis=-1, keepdims=True)
        p_a = jnp.exp2(lw_a).astype(BF16)
        p_b = jnp.exp2(jnp.where(has_prev, lw_b - c_a, -jnp.inf)).astype(BF16)
        acc_ref[hh] = _dot(p_a, v_ref[hh, 0, own_rows, :]) + _dot(p_b, v_ref[hh, 0, prev_rows, :])
        c_ref[hh] = c_a + jnp.where(has_prev, jnp.sum(nk_b, axis=-1, keepdims=True), 0.0)

    def cond(state):
        j, cmin = state
        return (j >= 0) & (cmin < SB_DEAD)

    def body(state):
        j, _ = state
        krows = pl.ds(pl.multiple_of(j * t, t), t)
        for hh in range(SB_HEADS):
            nk, lw = tile(q_ref[hh, 0], krows, hh, None)
            acc_ref[hh] += _dot(jnp.exp2(lw - c_ref[hh]).astype(BF16), v_ref[hh, 0, krows, :])
            c_ref[hh] += jnp.sum(nk, axis=-1, keepdims=True)
        return j - 1, jnp.min(c_ref[...])

    lax.while_loop(cond, body, (i - 2, jnp.min(c_ref[...])))
    o_ref[0] = jnp.concatenate([acc_ref[hh] for hh in range(SB_HEADS)], axis=1).astype(o_ref.dtype)


def _stickbreaking(qkv, *, batch, seq):
    nh = qkv.shape[0] // 3
    view = qkv.reshape(3 * nh, batch, seq, HEAD_DIM)
    g = SB_HEADS
    assert nh % g == 0
    return pl.pallas_call(
        _sb_kernel,
        grid=(batch, nh // g, seq // SB_T),
        in_specs=[
            pl.BlockSpec((g, 1, SB_T, HEAD_DIM), lambda b, h, i: (h, b, i, 0)),
            pl.BlockSpec((g, 1, seq, HEAD_DIM), lambda b, h, i: (nh // g + h, b, 0, 0)),
            pl.BlockSpec((g, 1, seq, HEAD_DIM), lambda b, h, i: (2 * (nh // g) + h, b, 0, 0)),
        ],
        out_specs=pl.BlockSpec((1, SB_T, g * HEAD_DIM), lambda b, h, i: (b, i, h)),
        out_shape=jax.ShapeDtypeStruct((batch, seq, nh * HEAD_DIM), BF16),
        scratch_shapes=[pltpu.VMEM((g, SB_T, 1), F32), pltpu.VMEM((g, SB_T, HEAD_DIM), F32)],
        compiler_params=_cparams(("parallel", "parallel", "arbitrary")),
        name="stickbreaking",
    )(view, view, view)


def kernel(x, ln_gains, ffn_w_gate, ffn_w_up, ffn_w_down, w_qkv_even, w_out_even, w_qkv_odd, w_out_odd,
           rel_bias, final_gain):
    batch, seq, d = x.shape
    depth = ln_gains.shape[0]
    m = batch * seq
    wg, wu, wd = (w.astype(BF16) for w in (ffn_w_gate, ffn_w_up, ffn_w_down))
    wqkv = (w_qkv_even.astype(BF16), w_qkv_odd.astype(BF16))
    wout = (w_out_even.astype(BF16), w_out_odd.astype(BF16))
    rel_bias = rel_bias.astype(F32)

    h = x.reshape(m, d)
    for i in range(depth):
        h = _ffn(h, ln_gains[i, 0], wg, wu, wd, i, 0)
        qkv = _qkv(h, ln_gains[i, 1], wqkv[i % 2][i // 2])
        if i % 2 == 0:
            parts = [_dilated(qkv, rel_bias, batch=batch, seq=seq), _moba(qkv, rel_bias, batch=batch, seq=seq)]
        else:
            parts = [_stickbreaking(qkv, batch=batch, seq=seq)]
        parts = [p.reshape(m, p.shape[-1]) for p in parts]
        h = _oproj(h, parts, wout[i % 2][i // 2])
        last = i == depth - 1
        h = _ffn(h, ln_gains[i, 2], wg, wu, wd, i, 1, final_gain if last else None)
    return h.reshape(batch, seq, d)
```

```python
import functools
import math

import jax
import jax.numpy as jnp
from jax import lax
from jax.experimental import pallas as pl
from jax.experimental.pallas import tpu as pltpu

F32 = jnp.float32
BF16 = jnp.bfloat16

HEAD_DIM = 128
N_HEADS = 16
N_HEADS_A = 8
DILATIONS = (1, 4, 16)
DIL_W = 128
MOBA_BLOCK = 256
MOBA_TOPK = 3
MOBA_GROUP = 1
LOG2E = math.log2(math.e)
MOBA_MASK = -(2.0 ** 100)
REL_BUCKETS = 32
REL_MAX_DIST = 2048
FFN_HALF = 0.5
RMS_EPS = 1e-6
NEG = -1e30
SB_T = 256
SB_HEADS = 8
SB_DEAD = 152.0
V7X_VMEM_LIMIT = 56 * 1024 * 1024


def _cparams(sem):
    return pltpu.CompilerParams(dimension_semantics=sem, vmem_limit_bytes=V7X_VMEM_LIMIT)


def _rms(x, g):
    ms = jnp.mean(x * x, axis=-1, keepdims=True)
    return x * lax.rsqrt(ms + RMS_EPS) * g


def _dot(a, b):
    return jnp.dot(a, b, preferred_element_type=F32)


def _serpentine(outer, inner, n_inner):
    return jnp.where(outer % 2 == 0, inner, n_inner - 1 - inner)


def _dot_nt(a, b):
    return lax.dot_general(a, b, (((1,), (1,)), ((), ())), preferred_element_type=F32)


def _ffn_kernel(h_ref, g_ref, wg_ref, wu_ref, wd_ref, *rest, nf, dn, final):
    if final:
        fg_ref, o_ref, n_ref = rest
    else:
        o_ref, n_ref = rest
    f = pl.program_id(1)

    @pl.when(f == 0)
    def _():
        n_ref[...] = _rms(h_ref[...], g_ref[...]).astype(BF16)
        o_ref[...] = jnp.zeros(o_ref.shape, F32)

    n = n_ref[...]
    gate = _dot(n, wg_ref[...])
    up = _dot(n, wu_ref[...])
    a = (gate * (1.0 / (1.0 + jnp.exp(-gate))) * up).astype(BF16)
    d = o_ref.shape[1]
    for c in range(d // dn):
        sl = slice(c * dn, (c + 1) * dn)
        o_ref[:, sl] += _dot(a, wd_ref[:, sl])

    @pl.when(f == nf - 1)
    def _():
        y = h_ref[...] + FFN_HALF * o_ref[...]
        if final:
            y = _rms(y, fg_ref[...])
        o_ref[...] = y


def _ffn(h, gain, wg, wu, wd, layer, half, final_gain=None, *, tm=512, tf=512, dn=512):
    m, d = h.shape
    dff = wg.shape[-1]
    nf = dff // tf
    final = final_gain is not None

    def chunk(i, f):
        return _serpentine(i, f, nf)

    in_specs = [
        pl.BlockSpec((tm, d), lambda i, f: (i, 0)),
        pl.BlockSpec((1, d), lambda i, f: (0, 0)),
        pl.BlockSpec((None, None, d, tf), lambda i, f: (layer, half, 0, chunk(i, f))),
        pl.BlockSpec((None, None, d, tf), lambda i, f: (layer, half, 0, chunk(i, f))),
        pl.BlockSpec((None, None, tf, d), lambda i, f: (layer, half, chunk(i, f), 0)),
    ]
    args = [h, gain.reshape(1, d), wg, wu, wd]
    if final:
        in_specs.append(pl.BlockSpec((1, d), lambda i, f: (0, 0)))
        args.append(final_gain.reshape(1, d))
    return pl.pallas_call(
        functools.partial(_ffn_kernel, nf=nf, dn=dn, final=final),
        grid=(m // tm, nf),
        in_specs=in_specs,
        out_specs=pl.BlockSpec((tm, d), lambda i, f: (i, 0)),
        out_shape=jax.ShapeDtypeStruct((m, d), F32),
        scratch_shapes=[pltpu.VMEM((tm, d), BF16)],
        compiler_params=_cparams(("parallel", "arbitrary")),
        name="ffn_final" if final else "ffn",
    )(*args)


def _qkv_kernel(h_ref, g_ref, w_ref, o_ref, n_ref, *, q_blocks, n_blocks, scale):
    @pl.when(pl.program_id(1) == 0)
    def _():
        n_ref[...] = _rms(h_ref[...], g_ref[...]).astype(BF16)

    col = _serpentine(pl.program_id(0), pl.program_id(1), n_blocks)
    acc = _dot(n_ref[...], w_ref[...])
    acc = acc * jnp.where(col < q_blocks, scale, 1.0).astype(F32)
    for c in range(o_ref.shape[0]):
        o_ref[c] = acc[:, c * HEAD_DIM:(c + 1) * HEAD_DIM].astype(BF16)


def _qkv(h, gain, w, *, tm=1024, tn=1024):
    m, d = h.shape
    n3 = w.shape[1]
    hpb = tn // HEAD_DIM
    nj = n3 // tn
    return pl.pallas_call(
        functools.partial(_qkv_kernel, q_blocks=(n3 // 3) // tn, n_blocks=nj, scale=HEAD_DIM ** -0.5 * LOG2E),
        grid=(m // tm, nj),
        in_specs=[
            pl.BlockSpec((tm, d), lambda i, j: (i, 0)),
            pl.BlockSpec((1, d), lambda i, j: (0, 0)),
            pl.BlockSpec((d, tn), lambda i, j: (0, _serpentine(i, j, nj))),
        ],
        out_specs=pl.BlockSpec((hpb, tm, HEAD_DIM), lambda i, j: (_serpentine(i, j, nj), i, 0)),
        out_shape=jax.ShapeDtypeStruct((n3 // HEAD_DIM, m, HEAD_DIM), BF16),
        scratch_shapes=[pltpu.VMEM((tm, d), BF16)],
        compiler_params=_cparams(("parallel", "arbitrary")),
        name="qkv",
    )(h, gain.reshape(1, d), w)


def _oproj_kernel(h_ref, *rest, n_in):
    a_refs = rest[:n_in]
    w_ref = rest[n_in]
    o_ref = rest[n_in + 1]
    acc = h_ref[...]
    off = 0
    for a_ref in a_refs:
        k = a_ref.shape[1]
        acc = acc + _dot(a_ref[...], w_ref[off:off + k, :])
        off += k
    o_ref[...] = acc


def _oproj(h, attn_parts, w, *, tm=512):
    m, d = h.shape
    in_specs = [pl.BlockSpec((tm, d), lambda i: (i, 0))]
    for a in attn_parts:
        in_specs.append(pl.BlockSpec((tm, a.shape[1]), lambda i: (i, 0)))
    in_specs.append(pl.BlockSpec(w.shape, lambda i: (0, 0)))
    return pl.pallas_call(
        functools.partial(_oproj_kernel, n_in=len(attn_parts)),
        grid=(m // tm,),
        in_specs=in_specs,
        out_specs=pl.BlockSpec((tm, d), lambda i: (i, 0)),
        out_shape=jax.ShapeDtypeStruct((m, d), F32),
        compiler_params=_cparams(("parallel",)),
        name="oproj",
    )(h, *attn_parts, w)


def _bucket_bias(dist, tab_ref, head):
    max_exact = REL_BUCKETS // 2
    d = jnp.maximum(dist, 0)
    df = jnp.maximum(d, 1).astype(F32)
    large = max_exact + (jnp.log(df / max_exact) / math.log(REL_MAX_DIST / max_exact)
                         * (REL_BUCKETS - max_exact)).astype(jnp.int32)
    large = jnp.minimum(large, REL_BUCKETS - 1)
    bucket = jnp.where(d < max_exact, d, large)
    out = jnp.full(dist.shape, tab_ref[0, head], F32)
    for b in range(1, REL_BUCKETS):
        out = jnp.where(bucket == b, tab_ref[b, head], out)
    return out


def _dilated_kernel(tab_ref, q_ref, k_ref, v_ref, o_ref, bias_ref, stage_ref, num_ref, lse_ref, *, seq, head0):
    nbr = len(DILATIONS)
    h = pl.program_id(0)
    b = pl.program_id(1)
    w = DIL_W
    j = lax.broadcasted_iota(jnp.int32, (w, 2 * w), 1)

    @pl.when(b == 0)
    def _():
        i = lax.broadcasted_iota(jnp.int32, (w, 2 * w), 0)
        rel = w + i - j
        ok = (rel >= 0) & (rel <= w)
        for br, r in enumerate(DILATIONS):
            bias_ref[br] = jnp.where(ok, _bucket_bias(rel * r, tab_ref, head0 + h) * LOG2E, NEG)

    no_prev = jnp.where(j < w, NEG, 0.0).astype(F32)

    for t, ref in enumerate((q_ref, k_ref, v_ref)):
        stage_ref[t] = ref[0, 0].astype(F32)

    for br, r in enumerate(DILATIONS):
        sub = seq // r
        nb = sub // w
        for c in range(r):
            rows = pl.ds(c, sub, stride=r) if r > 1 else slice(None)
            if r > 1:
                qc, kc, vc = (stage_ref[t, rows, :].astype(BF16) for t in range(3))
            else:
                qc, kc, vc = q_ref[0, 0], k_ref[0, 0], v_ref[0, 0]

            def band(t):
                prev = jnp.concatenate([t[:w], t[:sub - w]], axis=0) if nb > 1 else t
                return jnp.concatenate([prev.reshape(nb, w, HEAD_DIM), t.reshape(nb, w, HEAD_DIM)], axis=1)

            s = lax.dot_general(qc.reshape(nb, w, HEAD_DIM), band(kc), (((2,), (2,)), ((0,), (0,))),
                                preferred_element_type=F32) + bias_ref[br][None]
            s = jnp.concatenate([s[:1] + no_prev[None], s[1:]], axis=0) if nb > 1 else s + no_prev[None]
            m = jnp.max(s, axis=-1, keepdims=True)
            p = jnp.exp2(s - m)
            den = jnp.sum(p, axis=-1, keepdims=True)
            o = lax.dot_general(p.astype(BF16), band(vc), (((2,), (1,)), ((0,), (0,))),
                                preferred_element_type=F32) / den
            lse = m + jnp.log(den) * LOG2E
            num_ref[br, rows, :] = o.reshape(sub, HEAD_DIM)
            lse_ref[br, rows, :] = jnp.broadcast_to(lse, (nb, w, HEAD_DIM)).reshape(sub, HEAD_DIM)

    ch = 512

    def combine(t, carry):
        rows = pl.ds(pl.multiple_of(t * ch, ch), ch)
        ls = [lse_ref[br, rows, :] for br in range(nbr)]
        m = functools.reduce(jnp.maximum, ls)
        ws = [jnp.exp2(l - m) for l in ls]
        tot = functools.reduce(lambda x, y: x + y, ws)
        acc = functools.reduce(lambda x, y: x + y, [wt * num_ref[br, rows, :] for br, wt in enumerate(ws)])
        o_ref[0, rows, :] = (acc / tot).astype(o_ref.dtype)
        return carry

    lax.fori_loop(0, seq // ch, combine, 0)


def _dilated(qkv, rel_bias, *, batch, seq, head0=0, n_heads=N_HEADS_A):
    nh_all = qkv.shape[0] // 3
    view = qkv.reshape(3 * nh_all, batch, seq, HEAD_DIM)
    nbr = len(DILATIONS)
    return pl.pallas_call(
        functools.partial(_dilated_kernel, seq=seq, head0=head0),
        grid=(n_heads, batch),
        in_specs=[pl.BlockSpec(memory_space=pltpu.SMEM)] + [
            pl.BlockSpec((1, 1, seq, HEAD_DIM), lambda h, b, j=j: (j * nh_all + head0 + h, b, 0, 0))
            for j in range(3)],
        out_specs=pl.BlockSpec((1, seq, HEAD_DIM), lambda h, b: (b, 0, h)),
        out_shape=jax.ShapeDtypeStruct((batch, seq, n_heads * HEAD_DIM), BF16),
        scratch_shapes=[
            pltpu.VMEM((nbr, DIL_W, 2 * DIL_W), F32),
            pltpu.VMEM((3, seq, HEAD_DIM), F32),
            pltpu.VMEM((nbr, seq, HEAD_DIM), F32),
            pltpu.VMEM((nbr, seq, HEAD_DIM), F32),
        ],
        compiler_params=_cparams(("arbitrary", "arbitrary")),
        name="dilated",
    )(rel_bias, view, view, view)


def _moba_kernel(tab_ref, q_ref, k_ref, v_ref, o_ref, bias_ref, qa_ref, ka_ref, va_ref, *, seq, head0):
    blk = MOBA_BLOCK
    nblk = seq // blk
    hd = HEAD_DIM
    h = pl.program_id(0)
    b = pl.program_id(1)

    @pl.when(b == 0)
    def _():
        r = lax.broadcasted_iota(jnp.int32, (blk, blk), 0)
        c = lax.broadcasted_iota(jnp.int32, (blk, blk), 1)

        def fill(dlt, carry):
            dist = dlt * blk + r - c
            bias_ref[dlt] = jnp.where(dist >= 0, _bucket_bias(dist, tab_ref, head0 + h) * LOG2E, NEG)
            return carry

        lax.fori_loop(0, nblk, fill, 0)

    @pl.when((h == 0) & (b == 0))
    def _():
        key_blk = lax.broadcasted_iota(jnp.int32, (seq, hd), 0) // blk
        lane = lax.broadcasted_iota(jnp.int32, (seq, hd), 1)
        ka_ref[:, hd:] = jnp.where(lane == key_blk, MOBA_MASK, 0.0).astype(BF16)
        va_ref[:, hd:] = jnp.where(lane == 0, 1.0, 0.0).astype(BF16)

    q = q_ref[0, 0]
    k = k_ref[0, 0]
    qa_ref[:, :hd] = q
    ka_ref[:, :hd] = k
    va_ref[:, :hd] = v_ref[0, 0]

    kbar = jnp.sum(k.astype(F32).reshape(nblk, blk, hd), axis=1) * (1.0 / blk)
    k1 = kbar.astype(BF16)
    r1 = kbar - k1.astype(F32)
    k2 = r1.astype(BF16)
    k3 = (r1 - k2.astype(F32)).astype(BF16)
    gate = _dot_nt(k1, q) + _dot_nt(k2, q) + _dot_nt(k3, q)
    n_idx = lax.broadcasted_iota(jnp.int32, gate.shape, 0)
    own = lax.broadcasted_iota(jnp.int32, gate.shape, 1) // blk
    gate = jnp.where(n_idx < own, gate, -jnp.inf)
    skip = jnp.where(n_idx == own, 0.0, 1.0)
    for _ in range(min(MOBA_TOPK, nblk)):
        best = jnp.max(gate, axis=0, keepdims=True)
        idx = jnp.min(jnp.where(gate == best, n_idx, nblk), axis=0, keepdims=True)
        skip = jnp.where((n_idx == idx) & (best > -jnp.inf), 0.0, skip)
        gate = jnp.where(n_idx == idx, -jnp.inf, gate)
    skip = jnp.concatenate([skip, jnp.zeros((hd - nblk, seq), F32)], axis=0)
    qa_ref[:, hd:] = skip.T.astype(BF16)

    group = MOBA_GROUP
    for i in range(nblk):
        nkb = (i // group + 1) * group
        rows = slice(i * blk, (i + 1) * blk)
        s = _dot_nt(qa_ref[rows, :], ka_ref[:nkb * blk, :])
        s = s + jnp.concatenate([bias_ref[max(i - n, 0)] for n in range(nkb)], axis=1)
        m = jnp.max(s, axis=-1, keepdims=True)
        p = jnp.exp2(s - m).astype(BF16)
        o = _dot(p, va_ref[:nkb * blk, :])
        o_ref[0, rows, :] = (o[:, :hd] / o[:, hd:hd + 1]).astype(o_ref.dtype)


def _moba(qkv, rel_bias, *, batch, seq, head0=N_HEADS_A, n_heads=N_HEADS - N_HEADS_A):
    nh_all = qkv.shape[0] // 3
    blk = MOBA_BLOCK
    nblk = seq // blk
    assert nblk % MOBA_GROUP == 0 and nblk <= HEAD_DIM
    view = qkv.reshape(3 * nh_all, batch, seq, HEAD_DIM)
    return pl.pallas_call(
        functools.partial(_moba_kernel, seq=seq, head0=head0),
        grid=(n_heads, batch),
        in_specs=[pl.BlockSpec(memory_space=pltpu.SMEM)] + [
            pl.BlockSpec((1, 1, seq, HEAD_DIM), lambda h, b, j=j: (j * nh_all + head0 + h, b, 0, 0))
            for j in range(3)],
        out_specs=pl.BlockSpec((1, seq, HEAD_DIM), lambda h, b: (b, 0, h)),
        out_shape=jax.ShapeDtypeStruct((batch, seq, n_heads * HEAD_DIM), BF16),
        scratch_shapes=[
            pltpu.VMEM((nblk, blk, blk), F32),
            pltpu.VMEM((seq, 2 * HEAD_DIM), BF16),
            pltpu.VMEM((seq, 2 * HEAD_DIM), BF16),
            pltpu.VMEM((seq, 2 * HEAD_DIM), BF16),
        ],
        compiler_params=_cparams(("arbitrary", "arbitrary")),
        name="moba",
    )(rel_bias, view, view, view)


def _sb_kernel(q_ref, k_ref, v_ref, o_ref, c_ref, acc_ref):
    t = SB_T
    i = pl.program_id(2)
    rows = lax.broadcasted_iota(jnp.int32, (t, t), 0)
    cols = lax.broadcasted_iota(jnp.int32, (t, t), 1)
    causal = cols < rows
    suffix = jnp.where(rows > cols, 1.0, 0.0).astype(BF16)
    sign_bit = jnp.uint32(0x80000000)

    def tile(q, krows, hh, keep):
        z = _dot_nt(q, k_ref[hh, 0, krows, :])
        neg_abs = lax.bitcast_convert_type(lax.bitcast_convert_type(z, jnp.uint32) | sign_bit, F32)
        sp = jnp.maximum(z, 0.0) + jnp.log(1.0 + jnp.exp2(neg_abs)) * LOG2E
        nk = sp if keep is None else jnp.where(keep, sp, 0.0)
        logw = (z - sp) - _dot(nk.astype(BF16), suffix)
        if keep is not None:
            logw = jnp.where(keep, logw, -jnp.inf)
        return nk, logw

    has_prev = i > 0
    own_rows = pl.ds(pl.multiple_of(i * t, t), t)
    prev_rows = pl.ds(pl.multiple_of(jnp.maximum(i - 1, 0) * t, t), t)
    for hh in range(SB_HEADS):
        q = q_ref[hh, 0]
        nk_a, lw_a = tile(q, own_rows, hh, causal)
        nk_b, lw_b = tile(q, prev_rows, hh, None)
        c_a = jnp.sum(nk_a, axis=-1, keepdims=True)
        p_a = jnp.exp2(lw_a).astype(BF16)
        p_b = jnp.exp2(jnp.where(has_prev, lw_b - c_a, -jnp.inf)).astype(BF16)
        acc_ref[hh] = _dot(p_a, v_ref[hh, 0, own_rows, :]) + _dot(p_b, v_ref[hh, 0, prev_rows, :])
        c_ref[hh] = c_a + jnp.where(has_prev, jnp.sum(nk_b, axis=-1, keepdims=True), 0.0)

    def cond(state):
        j, cmin = state
        return (j >= 0) & (cmin < SB_DEAD)

    def body(state):
        j, _ = state
        krows = pl.ds(pl.multiple_of(j * t, t), t)
        for hh in range(SB_HEADS):
            nk, lw = tile(q_ref[hh, 0], krows, hh, None)
            acc_ref[hh] += _dot(jnp.exp2(lw - c_ref[hh]).astype(BF16), v_ref[hh, 0, krows, :])
            c_ref[hh] += jnp.sum(nk, axis=-1, keepdims=True)
        return j - 1, jnp.min(c_ref[...])

    lax.while_loop(cond, body, (i - 2, jnp.min(c_ref[...])))
    o_ref[0] = jnp.concatenate([acc_ref[hh] for hh in range(SB_HEADS)], axis=1).astype(o_ref.dtype)


def _stickbreaking(qkv, *, batch, seq):
    nh = qkv.shape[0] // 3
    view = qkv.reshape(3 * nh, batch, seq, HEAD_DIM)
    g = SB_HEADS
    assert nh % g == 0
    return pl.pallas_call(
        _sb_kernel,
        grid=(batch, nh // g, seq // SB_T),
        in_specs=[
            pl.BlockSpec((g, 1, SB_T, HEAD_DIM), lambda b, h, i: (h, b, i, 0)),
            pl.BlockSpec((g, 1, seq, HEAD_DIM), lambda b, h, i: (nh // g + h, b, 0, 0)),
            pl.BlockSpec((g, 1, seq, HEAD_DIM), lambda b, h, i: (2 * (nh // g) + h, b, 0, 0)),
        ],
        out_specs=pl.BlockSpec((1, SB_T, g * HEAD_DIM), lambda b, h, i: (b, i, h)),
        out_shape=jax.ShapeDtypeStruct((batch, seq, nh * HEAD_DIM), BF16),
        scratch_shapes=[pltpu.VMEM((g, SB_T, 1), F32), pltpu.VMEM((g, SB_T, HEAD_DIM), F32)],
        compiler_params=_cparams(("parallel", "parallel", "arbitrary")),
        name="stickbreaking",
    )(view, view, view)


def kernel(x, ln_gains, ffn_w_gate, ffn_w_up, ffn_w_down, w_qkv_even, w_out_even, w_qkv_odd, w_out_odd,
           rel_bias, final_gain):
    batch, seq, d = x.shape
    depth = ln_gains.shape[0]
    m = batch * seq
    wg, wu, wd = (w.astype(BF16) for w in (ffn_w_gate, ffn_w_up, ffn_w_down))
    wqkv = (w_qkv_even.astype(BF16), w_qkv_odd.astype(BF16))
    wout = (w_out_even.astype(BF16), w_out_odd.astype(BF16))
    rel_bias = rel_bias.astype(F32)

    h = x.reshape(m, d)
    for i in range(depth):
        h = _ffn(h, ln_gains[i, 0], wg, wu, wd, i, 0)
        qkv = _qkv(h, ln_gains[i, 1], wqkv[i % 2][i // 2])
        if i % 2 == 0:
            parts = [_dilated(qkv, rel_bias, batch=batch, seq=seq), _moba(qkv, rel_bias, batch=batch, seq=seq)]
        else:
            parts = [_stickbreaking(qkv, batch=batch, seq=seq)]
        parts = [p.reshape(m, p.shape[-1]) for p in parts]
        h = _oproj(h, parts, wout[i % 2][i // 2])
        last = i == depth - 1
        h = _ffn(h, ln_gains[i, 2], wg, wu, wd, i, 1, final_gain if last else None)
    return h.reshape(batch, seq, d)
```

```python
import functools
import math

import jax
import jax.numpy as jnp
from jax import lax
from jax.experimental import pallas as pl
from jax.experimental.pallas import tpu as pltpu

F32 = jnp.float32
BF16 = jnp.bfloat16

HEAD_DIM = 128
N_HEADS = 16
N_HEADS_A = 8
DILATIONS = (1, 4, 16)
DIL_W = 128
MOBA_BLOCK = 256
MOBA_TOPK = 3
MOBA_GROUP = 1
LOG2E = math.log2(math.e)
MOBA_MASK = -(2.0 ** 100)
REL_BUCKETS = 32
REL_MAX_DIST = 2048
FFN_HALF = 0.5
RMS_EPS = 1e-6
NEG = -1e30
SB_T = 256
SB_HEADS = 8
SB_DEAD = 152.0
V7X_VMEM_LIMIT = 56 * 1024 * 1024


def _cparams(sem):
    return pltpu.CompilerParams(dimension_semantics=sem, vmem_limit_bytes=V7X_VMEM_LIMIT)


def _rms(x, g):
    ms = jnp.mean(x * x, axis=-1, keepdims=True)
    return x * lax.rsqrt(ms + RMS_EPS) * g


def _dot(a, b):
    return jnp.dot(a, b, preferred_element_type=F32)


def _serpentine(outer, inner, n_inner):
    return jnp.where(outer % 2 == 0, inner, n_inner - 1 - inner)


def _dot_nt(a, b):
    return lax.dot_general(a, b, (((1,), (1,)), ((), ())), preferred_element_type=F32)


def _ffn_kernel(h_ref, g_ref, wg_ref, wu_ref, wd_ref, *rest, nf, tail, dn, final):
    if final:
        fg_ref, o_ref, n_ref = rest
    else:
        o_ref, n_ref = rest
    f = pl.program_id(1)
    tf = wg_ref.shape[1]
    d = o_ref.shape[1]

    @pl.when(f == 0)
    def _():
        n_ref[...] = _rms(h_ref[...], g_ref[...]).astype(BF16)
        o_ref[...] = jnp.zeros(o_ref.shape, F32)

    def chunk(width):
        n = n_ref[...]
        gate = _dot(n, wg_ref[:, :width])
        up = _dot(n, wu_ref[:, :width])
        a = (gate * (1.0 / (1.0 + jnp.exp(-gate))) * up).astype(BF16)
        for c in range(d // dn):
            sl = slice(c * dn, (c + 1) * dn)
            o_ref[:, sl] += _dot(a, wd_ref[:width, sl])

    if tail == tf:
        chunk(tf)
    else:
        pl.when(f < nf - 1)(lambda: chunk(tf))
        pl.when(f == nf - 1)(lambda: chunk(tail))

    @pl.when(f == nf - 1)
    def _():
        y = h_ref[...] + FFN_HALF * o_ref[...]
        if final:
            y = _rms(y, fg_ref[...])
        o_ref[...] = y


def _ffn(h, gain, wg, wu, wd, layer, half, final_gain=None, *, tm=512, tf=1024, dn=512):
    m, d = h.shape
    dff = wg.shape[-1]
    nf = pl.cdiv(dff, tf)
    tail = dff - (nf - 1) * tf
    final = final_gain is not None
    in_specs = [
        pl.BlockSpec((tm, d), lambda i, f: (i, 0)),
        pl.BlockSpec((1, d), lambda i, f: (0, 0)),
        pl.BlockSpec((None, None, d, tf), lambda i, f: (layer, half, 0, f)),
        pl.BlockSpec((None, None, d, tf), lambda i, f: (layer, half, 0, f)),
        pl.BlockSpec((None, None, tf, d), lambda i, f: (layer, half, f, 0)),
    ]
    args = [h, gain.reshape(1, d), wg, wu, wd]
    if final:
        in_specs.append(pl.BlockSpec((1, d), lambda i, f: (0, 0)))
        args.append(final_gain.reshape(1, d))
    return pl.pallas_call(
        functools.partial(_ffn_kernel, nf=nf, tail=tail, dn=dn, final=final),
        grid=(m // tm, nf),
        in_specs=in_specs,
        out_specs=pl.BlockSpec((tm, d), lambda i, f: (i, 0)),
        out_shape=jax.ShapeDtypeStruct((m, d), F32),
        scratch_shapes=[pltpu.VMEM((tm, d), BF16)],
        compiler_params=_cparams(("parallel", "arbitrary")),
        name="ffn_final" if final else "ffn",
    )(*args)


def _qkv_kernel(h_ref, g_ref, w_ref, o_ref, n_ref, *, q_blocks, n_blocks, scale):
    @pl.when(pl.program_id(1) == 0)
    def _():
        n_ref[...] = _rms(h_ref[...], g_ref[...]).astype(BF16)

    col = _serpentine(pl.program_id(0), pl.program_id(1), n_blocks)
    acc = _dot(n_ref[...], w_ref[...])
    acc = acc * jnp.where(col < q_blocks, scale, 1.0).astype(F32)
    for c in range(o_ref.shape[0]):
        o_ref[c] = acc[:, c * HEAD_DIM:(c + 1) * HEAD_DIM].astype(BF16)


def _qkv(h, gain, w, *, tm=1024, tn=1024):
    m, d = h.shape
    n3 = w.shape[1]
    hpb = tn // HEAD_DIM
    nj = n3 // tn
    return pl.pallas_call(
        functools.partial(_qkv_kernel, q_blocks=(n3 // 3) // tn, n_blocks=nj, scale=HEAD_DIM ** -0.5 * LOG2E),
        grid=(m // tm, nj),
        in_specs=[
            pl.BlockSpec((tm, d), lambda i, j: (i, 0)),
            pl.BlockSpec((1, d), lambda i, j: (0, 0)),
            pl.BlockSpec((d, tn), lambda i, j: (0, _serpentine(i, j, nj))),
        ],
        out_specs=pl.BlockSpec((hpb, tm, HEAD_DIM), lambda i, j: (_serpentine(i, j, nj), i, 0)),
        out_shape=jax.ShapeDtypeStruct((n3 // HEAD_DIM, m, HEAD_DIM), BF16),
        scratch_shapes=[pltpu.VMEM((tm, d), BF16)],
        compiler_params=_cparams(("parallel", "arbitrary")),
        name="qkv",
    )(h, gain.reshape(1, d), w)


def _oproj_kernel(h_ref, *rest, n_in):
    a_refs = rest[:n_in]
    w_ref = rest[n_in]
    o_ref = rest[n_in + 1]
    acc = h_ref[...]
    off = 0
    for a_ref in a_refs:
        k = a_ref.shape[1]
        acc = acc + _dot(a_ref[...], w_ref[off:off + k, :])
        off += k
    o_ref[...] = acc


def _oproj(h, attn_parts, w, *, tm=512):
    m, d = h.shape
    in_specs = [pl.BlockSpec((tm, d), lambda i: (i, 0))]
    for a in attn_parts:
        in_specs.append(pl.BlockSpec((tm, a.shape[1]), lambda i: (i, 0)))
    in_specs.append(pl.BlockSpec(w.shape, lambda i: (0, 0)))
    return pl.pallas_call(
        functools.partial(_oproj_kernel, n_in=len(attn_parts)),
        grid=(m // tm,),
        in_specs=in_specs,
        out_specs=pl.BlockSpec((tm, d), lambda i: (i, 0)),
        out_shape=jax.ShapeDtypeStruct((m, d), F32),
        compiler_params=_cparams(("parallel",)),
        name="oproj",
    )(h, *attn_parts, w)


def _bucket_bias(dist, tab_ref, head):
    max_exact = REL_BUCKETS // 2
    d = jnp.maximum(dist, 0)
    df = jnp.maximum(d, 1).astype(F32)
    large = max_exact + (jnp.log(df / max_exact) / math.log(REL_MAX_DIST / max_exact)
                         * (REL_BUCKETS - max_exact)).astype(jnp.int32)
    large = jnp.minimum(large, REL_BUCKETS - 1)
    bucket = jnp.where(d < max_exact, d, large)
    out = jnp.full(dist.shape, tab_ref[0, head], F32)
    for b in range(1, REL_BUCKETS):
        out = jnp.where(bucket == b, tab_ref[b, head], out)
    return out


def _dilated_kernel(tab_ref, q_ref, k_ref, v_ref, o_ref, bias_ref, stage_ref, num_ref, lse_ref, *, seq, head0):
    nbr = len(DILATIONS)
    h = pl.program_id(0)
    b = pl.program_id(1)
    w = DIL_W
    j = lax.broadcasted_iota(jnp.int32, (w, 2 * w), 1)

    @pl.when(b == 0)
    def _():
        i = lax.broadcasted_iota(jnp.int32, (w, 2 * w), 0)
        rel = w + i - j
        ok = (rel >= 0) & (rel <= w)
        for br, r in enumerate(DILATIONS):
            bias_ref[br] = jnp.where(ok, _bucket_bias(rel * r, tab_ref, head0 + h) * LOG2E, NEG)

    no_prev = jnp.where(j < w, NEG, 0.0).astype(F32)

    for t, ref in enumerate((q_ref, k_ref, v_ref)):
        stage_ref[t] = ref[0, 0].astype(F32)

    for br, r in enumerate(DILATIONS):
        sub = seq // r
        nb = sub // w
        for c in range(r):
            rows = pl.ds(c, sub, stride=r) if r > 1 else slice(None)
            if r > 1:
                qc, kc, vc = (stage_ref[t, rows, :].astype(BF16) for t in range(3))
            else:
                qc, kc, vc = q_ref[0, 0], k_ref[0, 0], v_ref[0, 0]

            def band(t):
                prev = jnp.concatenate([t[:w], t[:sub - w]], axis=0) if nb > 1 else t
                return jnp.concatenate([prev.reshape(nb, w, HEAD_DIM), t.reshape(nb, w, HEAD_DIM)], axis=1)

            s = lax.dot_general(qc.reshape(nb, w, HEAD_DIM), band(kc), (((2,), (2,)), ((0,), (0,))),
                                preferred_element_type=F32) + bias_ref[br][None]
            s = jnp.concatenate([s[:1] + no_prev[None], s[1:]], axis=0) if nb > 1 else s + no_prev[None]
            m = jnp.max(s, axis=-1, keepdims=True)
            p = jnp.exp2(s - m)
            den = jnp.sum(p, axis=-1, keepdims=True)
            o = lax.dot_general(p.astype(BF16), band(vc), (((2,), (1,)), ((0,), (0,))),
                                preferred_element_type=F32) / den
            lse = m + jnp.log(den) * LOG2E
            num_ref[br, rows, :] = o.reshape(sub, HEAD_DIM)
            lse_ref[br, rows, :] = jnp.broadcast_to(lse, (nb, w, HEAD_DIM)).reshape(sub, HEAD_DIM)

    ch = 512

    def combine(t, carry):
        rows = pl.ds(pl.multiple_of(t * ch, ch), ch)
        ls = [lse_ref[br, rows, :] for br in range(nbr)]
        m = functools.reduce(jnp.maximum, ls)
        ws = [jnp.exp2(l - m) for l in ls]
        tot = functools.reduce(lambda x, y: x + y, ws)
        acc = functools.reduce(lambda x, y: x + y, [wt * num_ref[br, rows, :] for br, wt in enumerate(ws)])
        o_ref[0, rows, :] = (acc / tot).astype(o_ref.dtype)
        return carry

    lax.fori_loop(0, seq // ch, combine, 0)


def _dilated(qkv, rel_bias, *, batch, seq, head0=0, n_heads=N_HEADS_A):
    nh_all = qkv.shape[0] // 3
    view = qkv.reshape(3 * nh_all, batch, seq, HEAD_DIM)
    nbr = len(DILATIONS)
    return pl.pallas_call(
        functools.partial(_dilated_kernel, seq=seq, head0=head0),
        grid=(n_heads, batch),
        in_specs=[pl.BlockSpec(memory_space=pltpu.SMEM)] + [
            pl.BlockSpec((1, 1, seq, HEAD_DIM), lambda h, b, j=j: (j * nh_all + head0 + h, b, 0, 0))
            for j in range(3)],
        out_specs=pl.BlockSpec((1, seq, HEAD_DIM), lambda h, b: (b, 0, h)),
        out_shape=jax.ShapeDtypeStruct((batch, seq, n_heads * HEAD_DIM), BF16),
        scratch_shapes=[
            pltpu.VMEM((nbr, DIL_W, 2 * DIL_W), F32),
            pltpu.VMEM((3, seq, HEAD_DIM), F32),
            pltpu.VMEM((nbr, seq, HEAD_DIM), F32),
            pltpu.VMEM((nbr, seq, HEAD_DIM), F32),
        ],
        compiler_params=_cparams(("arbitrary", "arbitrary")),
        name="dilated",
    )(rel_bias, view, view, view)


def _moba_kernel(tab_ref, q_ref, k_ref, v_ref, o_ref, bias_ref, qa_ref, ka_ref, va_ref, *, seq, head0):
    blk = MOBA_BLOCK
    nblk = seq // blk
    hd = HEAD_DIM
    h = pl.program_id(0)
    b = pl.program_id(1)

    @pl.when(b == 0)
    def _():
        r = lax.broadcasted_iota(jnp.int32, (blk, blk), 0)
        c = lax.broadcasted_iota(jnp.int32, (blk, blk), 1)

        def fill(dlt, carry):
            dist = dlt * blk + r - c
            bias_ref[dlt] = jnp.where(dist >= 0, _bucket_bias(dist, tab_ref, head0 + h) * LOG2E, NEG)
            return carry

        lax.fori_loop(0, nblk, fill, 0)

    @pl.when((h == 0) & (b == 0))
    def _():
        key_blk = lax.broadcasted_iota(jnp.int32, (seq, hd), 0) // blk
        lane = lax.broadcasted_iota(jnp.int32, (seq, hd), 1)
        ka_ref[:, hd:] = jnp.where(lane == key_blk, MOBA_MASK, 0.0).astype(BF16)
        va_ref[:, hd:] = jnp.where(lane == 0, 1.0, 0.0).astype(BF16)

    q = q_ref[0, 0]
    k = k_ref[0, 0]
    qa_ref[:, :hd] = q
    ka_ref[:, :hd] = k
    va_ref[:, :hd] = v_ref[0, 0]

    kbar = jnp.sum(k.astype(F32).reshape(nblk, blk, hd), axis=1) * (1.0 / blk)
    k1 = kbar.astype(BF16)
    r1 = kbar - k1.astype(F32)
    k2 = r1.astype(BF16)
    k3 = (r1 - k2.astype(F32)).astype(BF16)
    gate = _dot_nt(k1, q) + _dot_nt(k2, q) + _dot_nt(k3, q)
    n_idx = lax.broadcasted_iota(jnp.int32, gate.shape, 0)
    own = lax.broadcasted_iota(jnp.int32, gate.shape, 1) // blk
    gate = jnp.where(n_idx < own, gate, -jnp.inf)
    skip = jnp.where(n_idx == own, 0.0, 1.0)
    for _ in range(min(MOBA_TOPK, nblk)):
        best = jnp.max(gate, axis=0, keepdims=True)
        idx = jnp.min(jnp.where(gate == best, n_idx, nblk), axis=0, keepdims=True)
        skip = jnp.where((n_idx == idx) & (best > -jnp.inf), 0.0, skip)
        gate = jnp.where(n_idx == idx, -jnp.inf, gate)
    skip = jnp.concatenate([skip, jnp.zeros((hd - nblk, seq), F32)], axis=0)
    qa_ref[:, hd:] = skip.T.astype(BF16)

    group = MOBA_GROUP
    for i in range(nblk):
        nkb = (i // group + 1) * group
        rows = slice(i * blk, (i + 1) * blk)
        s = _dot_nt(qa_ref[rows, :], ka_ref[:nkb * blk, :])
        s = s + jnp.concatenate([bias_ref[max(i - n, 0)] for n in range(nkb)], axis=1)
        m = jnp.max(s, axis=-1, keepdims=True)
        p = jnp.exp2(s - m).astype(BF16)
        o = _dot(p, va_ref[:nkb * blk, :])
        o_ref[0, rows, :] = (o[:, :hd] / o[:, hd:hd + 1]).astype(o_ref.dtype)


def _moba(qkv, rel_bias, *, batch, seq, head0=N_HEADS_A, n_heads=N_HEADS - N_HEADS_A):
    nh_all = qkv.shape[0] // 3
    blk = MOBA_BLOCK
    nblk = seq // blk
    assert nblk % MOBA_GROUP == 0 and nblk <= HEAD_DIM
    view = qkv.reshape(3 * nh_all, batch, seq, HEAD_DIM)
    return pl.pallas_call(
        functools.partial(_moba_kernel, seq=seq, head0=head0),
        grid=(n_heads, batch),
        in_specs=[pl.BlockSpec(memory_space=pltpu.SMEM)] + [
            pl.BlockSpec((1, 1, seq, HEAD_DIM), lambda h, b, j=j: (j * nh_all + head0 + h, b, 0, 0))
            for j in range(3)],
        out_specs=pl.BlockSpec((1, seq, HEAD_DIM), lambda h, b: (b, 0, h)),
        out_shape=jax.ShapeDtypeStruct((batch, seq, n_heads * HEAD_DIM), BF16),
        scratch_shapes=[
            pltpu.VMEM((nblk, blk, blk), F32),
            pltpu.VMEM((seq, 2 * HEAD_DIM), BF16),
            pltpu.VMEM((seq, 2 * HEAD_DIM), BF16),
            pltpu.VMEM((seq, 2 * HEAD_DIM), BF16),
        ],
        compiler_params=_cparams(("arbitrary", "arbitrary")),
        name="moba",
    )(rel_bias, view, view, view)


def _sb_kernel(q_ref, k_ref, v_ref, o_ref, c_ref, acc_ref):
    t = SB_T
    i = pl.program_id(2)
    rows = lax.broadcasted_iota(jnp.int32, (t, t), 0)
    cols = lax.broadcasted_iota(jnp.int32, (t, t), 1)
    causal = cols < rows
    suffix = jnp.where(rows > cols, 1.0, 0.0).astype(BF16)
    sign_bit = jnp.uint32(0x80000000)

    def tile(q, krows, hh, keep):
        z = _dot_nt(q, k_ref[hh, 0, krows, :])
        neg_abs = lax.bitcast_convert_type(lax.bitcast_convert_type(z, jnp.uint32) | sign_bit, F32)
        sp = jnp.maximum(z, 0.0) + jnp.log(1.0 + jnp.exp2(neg_abs)) * LOG2E
        nk = sp if keep is None else jnp.where(keep, sp, 0.0)
        logw = (z - sp) - _dot(nk.astype(BF16), suffix)
        if keep is not None:
            logw = jnp.where(keep, logw, -jnp.inf)
        return nk, logw

    has_prev = i > 0
    own_rows = pl.ds(pl.multiple_of(i * t, t), t)
    prev_rows = pl.ds(pl.multiple_of(jnp.maximum(i - 1, 0) * t, t), t)
    for hh in range(SB_HEADS):
        q = q_ref[hh, 0]
        nk_a, lw_a = tile(q, own_rows, hh, causal)
        nk_b, lw_b = tile(q, prev_rows, hh, None)
        c_a = jnp.sum(nk_a, axis=-1, keepdims=True)
        p_a = jnp.exp2(lw_a).astype(BF16)
        p_b = jnp.exp2(jnp.where(has_prev, lw_b - c_a, -jnp.inf)).astype(BF16)
        acc_ref[hh] = _dot(p_a, v_ref[hh, 0, own_rows, :]) + _dot(p_b, v_ref[hh, 0, prev_rows, :])
        c_ref[hh] = c_a + jnp.where(has_prev, jnp.sum(nk_b, axis=-1, keepdims=True), 0.0)

    def cond(state):
        j, cmin = state
        return (j >= 0) & (cmin < SB_DEAD)

    def body(state):
        j, _ = state
        krows = pl.ds(pl.multiple_of(j * t, t), t)
        for hh in range(SB_HEADS):
            nk, lw = tile(q_ref[hh, 0], krows, hh, None)
            acc_ref[hh] += _dot(jnp.exp2(lw - c_ref[hh]).astype(BF16), v_ref[hh, 0, krows, :])
            c_ref[hh] += jnp.sum(nk, axis=-1, keepdims=True)
        return j - 1, jnp.min(c_ref[...])

    lax.while_loop(cond, body, (i - 2, jnp.min(c_ref[...])))
    o_ref[0] = jnp.concatenate([acc_ref[hh] for hh in range(SB_HEADS)], axis=1).astype(o_ref.dtype)


def _stickbreaking(qkv, *, batch, seq):
    nh = qkv.shape[0] // 3
    view = qkv.reshape(3 * nh, batch, seq, HEAD_DIM)
    g = SB_HEADS
    assert nh % g == 0
    return pl.pallas_call(
        _sb_kernel,
        grid=(batch, nh // g, seq // SB_T),
        in_specs=[
            pl.BlockSpec((g, 1, SB_T, HEAD_DIM), lambda b, h, i: (h, b, i, 0)),
            pl.BlockSpec((g, 1, seq, HEAD_DIM), lambda b, h, i: (nh // g + h, b, 0, 0)),
            pl.BlockSpec((g, 1, seq, HEAD_DIM), lambda b, h, i: (2 * (nh // g) + h, b, 0, 0)),
        ],
        out_specs=pl.BlockSpec((1, SB_T, g * HEAD_DIM), lambda b, h, i: (b, i, h)),
        out_shape=jax.ShapeDtypeStruct((batch, seq, nh * HEAD_DIM), BF16),
        scratch_shapes=[pltpu.VMEM((g, SB_T, 1), F32), pltpu.VMEM((g, SB_T, HEAD_DIM), F32)],
        compiler_params=_cparams(("parallel", "parallel", "arbitrary")),
        name="stickbreaking",
    )(view, view, view)


def kernel(x, ln_gains, ffn_w_gate, ffn_w_up, ffn_w_down, w_qkv_even, w_out_even, w_qkv_odd, w_out_odd,
           rel_bias, final_gain):
    batch, seq, d = x.shape
    depth = ln_gains.shape[0]
    m = batch * seq
    wg, wu, wd = (w.astype(BF16) for w in (ffn_w_gate, ffn_w_up, ffn_w_down))
    wqkv = (w_qkv_even.astype(BF16), w_qkv_odd.astype(BF16))
    wout = (w_out_even.astype(BF16), w_out_odd.astype(BF16))
    rel_bias = rel_bias.astype(F32)

    h = x.reshape(m, d)
    for i in range(depth):
        h = _ffn(h, ln_gains[i, 0], wg, wu, wd, i, 0)
        qkv = _qkv(h, ln_gains[i, 1], wqkv[i % 2][i // 2])
        if i % 2 == 0:
            parts = [_dilated(qkv, rel_bias, batch=batch, seq=seq), _moba(qkv, rel_bias, batch=batch, seq=seq)]
        else:
            parts = [_stickbreaking(qkv, batch=batch, seq=seq)]
        parts = [p.reshape(m, p.shape[-1]) for p in parts]
        h = _oproj(h, parts, wout[i % 2][i // 2])
        last = i == depth - 1
        h = _ffn(h, ln_gains[i, 2], wg, wu, wd, i, 1, final_gain if last else None)
    return h.reshape(batch, seq, d)
```

```python
import functools
import math

import jax
import jax.numpy as jnp
from jax import lax
from jax.experimental import pallas as pl
from jax.experimental.pallas import tpu as pltpu

F32 = jnp.float32
BF16 = jnp.bfloat16

HEAD_DIM = 128
N_HEADS = 16
N_HEADS_A = 8
DILATIONS = (1, 4, 16)
DIL_W = 128
MOBA_BLOCK = 256
MOBA_TOPK = 3
MOBA_GROUP = 1
LOG2E = math.log2(math.e)
MOBA_MASK = -(2.0 ** 100)
REL_BUCKETS = 32
REL_MAX_DIST = 2048
FFN_HALF = 0.5
RMS_EPS = 1e-6
NEG = -1e30
SB_T = 256
SB_HEADS = 8
SB_DEAD = 152.0
V7X_VMEM_LIMIT = 56 * 1024 * 1024


def _cparams(sem):
    return pltpu.CompilerParams(dimension_semantics=sem, vmem_limit_bytes=V7X_VMEM_LIMIT)


def _rms(x, g):
    ms = jnp.mean(x * x, axis=-1, keepdims=True)
    return x * lax.rsqrt(ms + RMS_EPS) * g


def _dot(a, b):
    return jnp.dot(a, b, preferred_element_type=F32)


def _serpentine(outer, inner, n_inner):
    return jnp.where(outer % 2 == 0, inner, n_inner - 1 - inner)


def _dot_nt(a, b):
    return lax.dot_general(a, b, (((1,), (1,)), ((), ())), preferred_element_type=F32)


def _ffn_kernel(h_ref, g_ref, wg_ref, wu_ref, wd_ref, *rest, nf, tail, tail_step, dn, final):
    if final:
        fg_ref, o_ref, n_ref = rest
    else:
        o_ref, n_ref = rest
    f = pl.program_id(1)
    tf = wg_ref.shape[1]
    d = o_ref.shape[1]

    @pl.when(f == 0)
    def _():
        n_ref[...] = _rms(h_ref[...], g_ref[...]).astype(BF16)
        o_ref[...] = jnp.zeros(o_ref.shape, F32)

    def chunk(width):
        n = n_ref[...]
        gate = _dot(n, wg_ref[:, :width])
        up = _dot(n, wu_ref[:, :width])
        a = (gate * (1.0 / (1.0 + jnp.exp(-gate))) * up).astype(BF16)
        for c in range(d // dn):
            sl = slice(c * dn, (c + 1) * dn)
            o_ref[:, sl] += _dot(a, wd_ref[:width, sl])

    if tail == tf:
        chunk(tf)
    else:
        pl.when(f != tail_step)(lambda: chunk(tf))
        pl.when(f == tail_step)(lambda: chunk(tail))

    @pl.when(f == nf - 1)
    def _():
        y = h_ref[...] + FFN_HALF * o_ref[...]
        if final:
            y = _rms(y, fg_ref[...])
        o_ref[...] = y


def _ffn(h, gain, wg, wu, wd, layer, half, final_gain=None, *, tm=512, tf=1024, dn=512):
    m, d = h.shape
    dff = wg.shape[-1]
    nf = pl.cdiv(dff, tf)
    tail = dff - (nf - 1) * tf
    tail_step = (nf - 1) // 2
    final = final_gain is not None

    def chunk(f):
        return jnp.where(f == tail_step, nf - 1, jnp.where(f < tail_step, f, f - 1))

    in_specs = [
        pl.BlockSpec((tm, d), lambda i, f: (i, 0)),
        pl.BlockSpec((1, d), lambda i, f: (0, 0)),
        pl.BlockSpec((None, None, d, tf), lambda i, f: (layer, half, 0, chunk(f))),
        pl.BlockSpec((None, None, d, tf), lambda i, f: (layer, half, 0, chunk(f))),
        pl.BlockSpec((None, None, tf, d), lambda i, f: (layer, half, chunk(f), 0)),
    ]
    args = [h, gain.reshape(1, d), wg, wu, wd]
    if final:
        in_specs.append(pl.BlockSpec((1, d), lambda i, f: (0, 0)))
        args.append(final_gain.reshape(1, d))
    return pl.pallas_call(
        functools.partial(_ffn_kernel, nf=nf, tail=tail, tail_step=tail_step, dn=dn, final=final),
        grid=(m // tm, nf),
        in_specs=in_specs,
        out_specs=pl.BlockSpec((tm, d), lambda i, f: (i, 0)),
        out_shape=jax.ShapeDtypeStruct((m, d), F32),
        scratch_shapes=[pltpu.VMEM((tm, d), BF16)],
        compiler_params=_cparams(("parallel", "arbitrary")),
        name="ffn_final" if final else "ffn",
    )(*args)


def _qkv_kernel(h_ref, g_ref, w_ref, o_ref, n_ref, *, q_blocks, n_blocks, scale):
    @pl.when(pl.program_id(1) == 0)
    def _():
        n_ref[...] = _rms(h_ref[...], g_ref[...]).astype(BF16)

    col = _serpentine(pl.program_id(0), pl.program_id(1), n_blocks)
    acc = _dot(n_ref[...], w_ref[...])
    acc = acc * jnp.where(col < q_blocks, scale, 1.0).astype(F32)
    for c in range(o_ref.shape[0]):
        o_ref[c] = acc[:, c * HEAD_DIM:(c + 1) * HEAD_DIM].astype(BF16)


def _qkv(h, gain, w, *, tm=1024, tn=1024):
    m, d = h.shape
    n3 = w.shape[1]
    hpb = tn // HEAD_DIM
    nj = n3 // tn
    return pl.pallas_call(
        functools.partial(_qkv_kernel, q_blocks=(n3 // 3) // tn, n_blocks=nj, scale=HEAD_DIM ** -0.5 * LOG2E),
        grid=(m // tm, nj),
        in_specs=[
            pl.BlockSpec((tm, d), lambda i, j: (i, 0)),
            pl.BlockSpec((1, d), lambda i, j: (0, 0)),
            pl.BlockSpec((d, tn), lambda i, j: (0, _serpentine(i, j, nj))),
        ],
        out_specs=pl.BlockSpec((hpb, tm, HEAD_DIM), lambda i, j: (_serpentine(i, j, nj), i, 0)),
        out_shape=jax.ShapeDtypeStruct((n3 // HEAD_DIM, m, HEAD_DIM), BF16),
        scratch_shapes=[pltpu.VMEM((tm, d), BF16)],
        compiler_params=_cparams(("parallel", "arbitrary")),
        name="qkv",
    )(h, gain.reshape(1, d), w)


def _oproj_kernel(h_ref, *rest, n_in):
    a_refs = rest[:n_in]
    w_ref = rest[n_in]
    o_ref = rest[n_in + 1]
    acc = h_ref[...]
    off = 0
    for a_ref in a_refs:
        k = a_ref.shape[1]
        acc = acc + _dot(a_ref[...], w_ref[off:off + k, :])
        off += k
    o_ref[...] = acc


def _oproj(h, attn_parts, w, *, tm=512):
    m, d = h.shape
    in_specs = [pl.BlockSpec((tm, d), lambda i: (i, 0))]
    for a in attn_parts:
        in_specs.append(pl.BlockSpec((tm, a.shape[1]), lambda i: (i, 0)))
    in_specs.append(pl.BlockSpec(w.shape, lambda i: (0, 0)))
    return pl.pallas_call(
        functools.partial(_oproj_kernel, n_in=len(attn_parts)),
        grid=(m // tm,),
        in_specs=in_specs,
        out_specs=pl.BlockSpec((tm, d), lambda i: (i, 0)),
        out_shape=jax.ShapeDtypeStruct((m, d), F32),
        compiler_params=_cparams(("parallel",)),
        name="oproj",
    )(h, *attn_parts, w)


def _bucket_bias(dist, tab_ref, head):
    max_exact = REL_BUCKETS // 2
    d = jnp.maximum(dist, 0)
    df = jnp.maximum(d, 1).astype(F32)
    large = max_exact + (jnp.log(df / max_exact) / math.log(REL_MAX_DIST / max_exact)
                         * (REL_BUCKETS - max_exact)).astype(jnp.int32)
    large = jnp.minimum(large, REL_BUCKETS - 1)
    bucket = jnp.where(d < max_exact, d, large)
    out = jnp.full(dist.shape, tab_ref[0, head], F32)
    for b in range(1, REL_BUCKETS):
        out = jnp.where(bucket == b, tab_ref[b, head], out)
    return out


def _dilated_kernel(tab_ref, q_ref, k_ref, v_ref, o_ref, bias_ref, stage_ref, num_ref, lse_ref, *, seq, head0):
    nbr = len(DILATIONS)
    h = pl.program_id(0)
    b = pl.program_id(1)
    w = DIL_W
    j = lax.broadcasted_iota(jnp.int32, (w, 2 * w), 1)

    @pl.when(b == 0)
    def _():
        i = lax.broadcasted_iota(jnp.int32, (w, 2 * w), 0)
        rel = w + i - j
        ok = (rel >= 0) & (rel <= w)
        for br, r in enumerate(DILATIONS):
            bias_ref[br] = jnp.where(ok, _bucket_bias(rel * r, tab_ref, head0 + h) * LOG2E, NEG)

    no_prev = jnp.where(j < w, NEG, 0.0).astype(F32)

    for t, ref in enumerate((q_ref, k_ref, v_ref)):
        stage_ref[t] = ref[0, 0].astype(F32)

    for br, r in enumerate(DILATIONS):
        sub = seq // r
        nb = sub // w
        for c in range(r):
            rows = pl.ds(c, sub, stride=r) if r > 1 else slice(None)
            if r > 1:
                qc, kc, vc = (stage_ref[t, rows, :].astype(BF16) for t in range(3))
            else:
                qc, kc, vc = q_ref[0, 0], k_ref[0, 0], v_ref[0, 0]

            def band(t):
                prev = jnp.concatenate([t[:w], t[:sub - w]], axis=0) if nb > 1 else t
                return jnp.concatenate([prev.reshape(nb, w, HEAD_DIM), t.reshape(nb, w, HEAD_DIM)], axis=1)

            s = lax.dot_general(qc.reshape(nb, w, HEAD_DIM), band(kc), (((2,), (2,)), ((0,), (0,))),
                                preferred_element_type=F32) + bias_ref[br][None]
            s = jnp.concatenate([s[:1] + no_prev[None], s[1:]], axis=0) if nb > 1 else s + no_prev[None]
            m = jnp.max(s, axis=-1, keepdims=True)
            p = jnp.exp2(s - m)
            den = jnp.sum(p, axis=-1, keepdims=True)
            o = lax.dot_general(p.astype(BF16), band(vc), (((2,), (1,)), ((0,), (0,))),
                                preferred_element_type=F32) / den
            lse = m + jnp.log(den) * LOG2E
            num_ref[br, rows, :] = o.reshape(sub, HEAD_DIM)
            lse_ref[br, rows, :] = jnp.broadcast_to(lse, (nb, w, HEAD_DIM)).reshape(sub, HEAD_DIM)

    ch = 512

    def combine(t, carry):
        rows = pl.ds(pl.multiple_of(t * ch, ch), ch)
        ls = [lse_ref[br, rows, :] for br in range(nbr)]
        m = functools.reduce(jnp.maximum, ls)
        ws = [jnp.exp2(l - m) for l in ls]
        tot = functools.reduce(lambda x, y: x + y, ws)
        acc = functools.reduce(lambda x, y: x + y, [wt * num_ref[br, rows, :] for br, wt in enumerate(ws)])
        o_ref[0, rows, :] = (acc / tot).astype(o_ref.dtype)
        return carry

    lax.fori_loop(0, seq // ch, combine, 0)


def _dilated(qkv, rel_bias, *, batch, seq, head0=0, n_heads=N_HEADS_A):
    nh_all = qkv.shape[0] // 3
    view = qkv.reshape(3 * nh_all, batch, seq, HEAD_DIM)
    nbr = len(DILATIONS)
    return pl.pallas_call(
        functools.partial(_dilated_kernel, seq=seq, head0=head0),
        grid=(n_heads, batch),
        in_specs=[pl.BlockSpec(memory_space=pltpu.SMEM)] + [
            pl.BlockSpec((1, 1, seq, HEAD_DIM), lambda h, b, j=j: (j * nh_all + head0 + h, b, 0, 0))
            for j in range(3)],
        out_specs=pl.BlockSpec((1, seq, HEAD_DIM), lambda h, b: (b, 0, h)),
        out_shape=jax.ShapeDtypeStruct((batch, seq, n_heads * HEAD_DIM), BF16),
        scratch_shapes=[
            pltpu.VMEM((nbr, DIL_W, 2 * DIL_W), F32),
            pltpu.VMEM((3, seq, HEAD_DIM), F32),
            pltpu.VMEM((nbr, seq, HEAD_DIM), F32),
            pltpu.VMEM((nbr, seq, HEAD_DIM), F32),
        ],
        compiler_params=_cparams(("arbitrary", "arbitrary")),
        name="dilated",
    )(rel_bias, view, view, view)


def _moba_kernel(tab_ref, q_ref, k_ref, v_ref, o_ref, bias_ref, qa_ref, ka_ref, va_ref, *, seq, head0):
    blk = MOBA_BLOCK
    nblk = seq // blk
    hd = HEAD_DIM
    h = pl.program_id(0)
    b = pl.program_id(1)

    @pl.when(b == 0)
    def _():
        r = lax.broadcasted_iota(jnp.int32, (blk, blk), 0)
        c = lax.broadcasted_iota(jnp.int32, (blk, blk), 1)

        def fill(dlt, carry):
            dist = dlt * blk + r - c
            bias_ref[dlt] = jnp.where(dist >= 0, _bucket_bias(dist, tab_ref, head0 + h) * LOG2E, NEG)
            return carry

        lax.fori_loop(0, nblk, fill, 0)

    @pl.when((h == 0) & (b == 0))
    def _():
        key_blk = lax.broadcasted_iota(jnp.int32, (seq, hd), 0) // blk
        lane = lax.broadcasted_iota(jnp.int32, (seq, hd), 1)
        ka_ref[:, hd:] = jnp.where(lane == key_blk, MOBA_MASK, 0.0).astype(BF16)
        va_ref[:, hd:] = jnp.where(lane == 0, 1.0, 0.0).astype(BF16)

    q = q_ref[0, 0]
    k = k_ref[0, 0]
    qa_ref[:, :hd] = q
    ka_ref[:, :hd] = k
    va_ref[:, :hd] = v_ref[0, 0]

    kbar = jnp.sum(k.astype(F32).reshape(nblk, blk, hd), axis=1) * (1.0 / blk)
    k1 = kbar.astype(BF16)
    r1 = kbar - k1.astype(F32)
    k2 = r1.astype(BF16)
    k3 = (r1 - k2.astype(F32)).astype(BF16)
    gate = _dot_nt(k1, q) + _dot_nt(k2, q) + _dot_nt(k3, q)
    n_idx = lax.broadcasted_iota(jnp.int32, gate.shape, 0)
    own = lax.broadcasted_iota(jnp.int32, gate.shape, 1) // blk
    gate = jnp.where(n_idx < own, gate, -jnp.inf)
    skip = jnp.where(n_idx == own, 0.0, 1.0)
    for _ in range(min(MOBA_TOPK, nblk)):
        best = jnp.max(gate, axis=0, keepdims=True)
        idx = jnp.min(jnp.where(gate == best, n_idx, nblk), axis=0, keepdims=True)
        skip = jnp.where((n_idx == idx) & (best > -jnp.inf), 0.0, skip)
        gate = jnp.where(n_idx == idx, -jnp.inf, gate)
    skip = jnp.concatenate([skip, jnp.zeros((hd - nblk, seq), F32)], axis=0)
    qa_ref[:, hd:] = skip.T.astype(BF16)

    group = MOBA_GROUP
    for i in range(nblk):
        nkb = (i // group + 1) * group
        rows = slice(i * blk, (i + 1) * blk)
        s = _dot_nt(qa_ref[rows, :], ka_ref[:nkb * blk, :])
        s = s + jnp.concatenate([bias_ref[max(i - n, 0)] for n in range(nkb)], axis=1)
        m = jnp.max(s, axis=-1, keepdims=True)
        p = jnp.exp2(s - m).astype(BF16)
        o = _dot(p, va_ref[:nkb * blk, :])
        o_ref[0, rows, :] = (o[:, :hd] / o[:, hd:hd + 1]).astype(o_ref.dtype)


def _moba(qkv, rel_bias, *, batch, seq, head0=N_HEADS_A, n_heads=N_HEADS - N_HEADS_A):
    nh_all = qkv.shape[0] // 3
    blk = MOBA_BLOCK
    nblk = seq // blk
    assert nblk % MOBA_GROUP == 0 and nblk <= HEAD_DIM
    view = qkv.reshape(3 * nh_all, batch, seq, HEAD_DIM)
    return pl.pallas_call(
        functools.partial(_moba_kernel, seq=seq, head0=head0),
        grid=(n_heads, batch),
        in_specs=[pl.BlockSpec(memory_space=pltpu.SMEM)] + [
            pl.BlockSpec((1, 1, seq, HEAD_DIM), lambda h, b, j=j: (j * nh_all + head0 + h, b, 0, 0))
            for j in range(3)],
        out_specs=pl.BlockSpec((1, seq, HEAD_DIM), lambda h, b: (b, 0, h)),
        out_shape=jax.ShapeDtypeStruct((batch, seq, n_heads * HEAD_DIM), BF16),
        scratch_shapes=[
            pltpu.VMEM((nblk, blk, blk), F32),
            pltpu.VMEM((seq, 2 * HEAD_DIM), BF16),
            pltpu.VMEM((seq, 2 * HEAD_DIM), BF16),
            pltpu.VMEM((seq, 2 * HEAD_DIM), BF16),
        ],
        compiler_params=_cparams(("arbitrary", "arbitrary")),
        name="moba",
    )(rel_bias, view, view, view)


def _sb_kernel(q_ref, k_ref, v_ref, o_ref, c_ref, acc_ref):
    t = SB_T
    i = pl.program_id(2)
    rows = lax.broadcasted_iota(jnp.int32, (t, t), 0)
    cols = lax.broadcasted_iota(jnp.int32, (t, t), 1)
    causal = cols < rows
    suffix = jnp.where(rows > cols, 1.0, 0.0).astype(BF16)
    sign_bit = jnp.uint32(0x80000000)

    def tile(q, krows, hh, keep):
        z = _dot_nt(q, k_ref[hh, 0, krows, :])
        neg_abs = lax.bitcast_convert_type(lax.bitcast_convert_type(z, jnp.uint32) | sign_bit, F32)
        sp = jnp.maximum(z, 0.0) + jnp.log(1.0 + jnp.exp2(neg_abs)) * LOG2E
        nk = sp if keep is None else jnp.where(keep, sp, 0.0)
        logw = (z - sp) - _dot(nk.astype(BF16), suffix)
        if keep is not None:
            logw = jnp.where(keep, logw, -jnp.inf)
        return nk, logw

    has_prev = i > 0
    own_rows = pl.ds(pl.multiple_of(i * t, t), t)
    prev_rows = pl.ds(pl.multiple_of(jnp.maximum(i - 1, 0) * t, t), t)
    for hh in range(SB_HEADS):
        q = q_ref[hh, 0]
        nk_a, lw_a = tile(q, own_rows, hh, causal)
        nk_b, lw_b = tile(q, prev_rows, hh, None)
        c_a = jnp.sum(nk_a, axis=-1, keepdims=True)
        p_a = jnp.exp2(lw_a).astype(BF16)
        p_b = jnp.exp2(jnp.where(has_prev, lw_b - c_a, -jnp.inf)).astype(BF16)
        acc_ref[hh] = _dot(p_a, v_ref[hh, 0, own_rows, :]) + _dot(p_b, v_ref[hh, 0, prev_rows, :])
        c_ref[hh] = c_a + jnp.where(has_prev, jnp.sum(nk_b, axis=-1, keepdims=True), 0.0)

    def cond(state):
        j, cmin = state
        return (j >= 0) & (cmin < SB_DEAD)

    def body(state):
        j, _ = state
        krows = pl.ds(pl.multiple_of(j * t, t), t)
        for hh in range(SB_HEADS):
            nk, lw = tile(q_ref[hh, 0], krows, hh, None)
            acc_ref[hh] += _dot(jnp.exp2(lw - c_ref[hh]).astype(BF16), v_ref[hh, 0, krows, :])
            c_ref[hh] += jnp.sum(nk, axis=-1, keepdims=True)
        return j - 1, jnp.min(c_ref[...])

    lax.while_loop(cond, body, (i - 2, jnp.min(c_ref[...])))
    o_ref[0] = jnp.concatenate([acc_ref[hh] for hh in range(SB_HEADS)], axis=1).astype(o_ref.dtype)


def _stickbreaking(qkv, *, batch, seq):
    nh = qkv.shape[0] // 3
    view = qkv.reshape(3 * nh, batch, seq, HEAD_DIM)
    g = SB_HEADS
    assert nh % g == 0
    return pl.pallas_call(
        _sb_kernel,
        grid=(batch, nh // g, seq // SB_T),
        in_specs=[
            pl.BlockSpec((g, 1, SB_T, HEAD_DIM), lambda b, h, i: (h, b, i, 0)),
            pl.BlockSpec((g, 1, seq, HEAD_DIM), lambda b, h, i: (nh // g + h, b, 0, 0)),
            pl.BlockSpec((g, 1, seq, HEAD_DIM), lambda b, h, i: (2 * (nh // g) + h, b, 0, 0)),
        ],
        out_specs=pl.BlockSpec((1, SB_T, g * HEAD_DIM), lambda b, h, i: (b, i, h)),
        out_shape=jax.ShapeDtypeStruct((batch, seq, nh * HEAD_DIM), BF16),
        scratch_shapes=[pltpu.VMEM((g, SB_T, 1), F32), pltpu.VMEM((g, SB_T, HEAD_DIM), F32)],
        compiler_params=_cparams(("parallel", "parallel", "arbitrary")),
        name="stickbreaking",
    )(view, view, view)


def kernel(x, ln_gains, ffn_w_gate, ffn_w_up, ffn_w_down, w_qkv_even, w_out_even, w_qkv_odd, w_out_odd,
           rel_bias, final_gain):
    batch, seq, d = x.shape
    depth = ln_gains.shape[0]
    m = batch * seq
    wg, wu, wd = (w.astype(BF16) for w in (ffn_w_gate, ffn_w_up, ffn_w_down))
    wqkv = (w_qkv_even.astype(BF16), w_qkv_odd.astype(BF16))
    wout = (w_out_even.astype(BF16), w_out_odd.astype(BF16))
    rel_bias = rel_bias.astype(F32)

    h = x.reshape(m, d)
    for i in range(depth):
        h = _ffn(h, ln_gains[i, 0], wg, wu, wd, i, 0)
        qkv = _qkv(h, ln_gains[i, 1], wqkv[i % 2][i // 2])
        if i % 2 == 0:
            parts = [_dilated(qkv, rel_bias, batch=batch, seq=seq), _moba(qkv, rel_bias, batch=batch, seq=seq)]
        else:
            parts = [_stickbreaking(qkv, batch=batch, seq=seq)]
        parts = [p.reshape(m, p.shape[-1]) for p in parts]
        h = _oproj(h, parts, wout[i % 2][i // 2])
        last = i == depth - 1
        h = _ffn(h, ln_gains[i, 2], wg, wu, wd, i, 1, final_gain if last else None)
    return h.reshape(batch, seq, d)
```

```python
import functools
import math

import jax
import jax.numpy as jnp
from jax import lax
from jax.experimental import pallas as pl
from jax.experimental.pallas import tpu as pltpu

F32 = jnp.float32
BF16 = jnp.bfloat16

HEAD_DIM = 128
N_HEADS = 16
N_HEADS_A = 8
DILATIONS = (1, 4, 16)
DIL_W = 128
DIL_MID = 4
MOBA_BLOCK = 256
MOBA_TOPK = 3
LOG2E = math.log2(math.e)
MOBA_MASK = -(2.0 ** 100)
REL_BUCKETS = 32
REL_MAX_DIST = 2048
FFN_HALF = 0.5
RMS_EPS = 1e-6
NEG = -1e30
SB_T = 256
SB_HEADS = 8
SB_DEAD = 152.0
V7X_VMEM_LIMIT = 56 * 1024 * 1024


def _cparams(sem):
    return pltpu.CompilerParams(dimension_semantics=sem, vmem_limit_bytes=V7X_VMEM_LIMIT)


def _rms(x, g):
    ms = jnp.mean(x * x, axis=-1, keepdims=True)
    return x * lax.rsqrt(ms + RMS_EPS) * g


def _dot(a, b):
    return jnp.dot(a, b, preferred_element_type=F32)


def _serpentine(outer, inner, n_inner):
    return jnp.where(outer % 2 == 0, inner, n_inner - 1 - inner)


def _dot_nt(a, b):
    return lax.dot_general(a, b, (((1,), (1,)), ((), ())), preferred_element_type=F32)


def _ffn_kernel(h_ref, g_ref, wg_ref, wu_ref, wd_ref, *rest, nf, tail, tail_step, dn, final):
    if final:
        fg_ref, o_ref, n_ref = rest
    else:
        o_ref, n_ref = rest
    f = pl.program_id(1)
    tf = wg_ref.shape[1]
    d = o_ref.shape[1]

    @pl.when(f == 0)
    def _():
        n_ref[...] = _rms(h_ref[...], g_ref[...]).astype(BF16)
        o_ref[...] = jnp.zeros(o_ref.shape, F32)

    def chunk(width):
        n = n_ref[...]
        gate = _dot(n, wg_ref[:, :width])
        up = _dot(n, wu_ref[:, :width])
        a = (gate * (1.0 / (1.0 + jnp.exp(-gate))) * up).astype(BF16)
        for c in range(d // dn):
            sl = slice(c * dn, (c + 1) * dn)
            o_ref[:, sl] += _dot(a, wd_ref[:width, sl])

    if tail == tf:
        chunk(tf)
    else:
        pl.when(f != tail_step)(lambda: chunk(tf))
        pl.when(f == tail_step)(lambda: chunk(tail))

    @pl.when(f == nf - 1)
    def _():
        y = h_ref[...] + FFN_HALF * o_ref[...]
        if final:
            y = _rms(y, fg_ref[...])
        o_ref[...] = y


def _ffn(h, gain, wg, wu, wd, layer, half, final_gain=None, *, tm=512, tf=1024, dn=512):
    m, d = h.shape
    dff = wg.shape[-1]
    nf = pl.cdiv(dff, tf)
    tail = dff - (nf - 1) * tf
    tail_step = (nf - 1) // 2
    final = final_gain is not None

    def chunk(f):
        return jnp.where(f == tail_step, nf - 1, jnp.where(f < tail_step, f, f - 1))

    in_specs = [
        pl.BlockSpec((tm, d), lambda i, f: (i, 0)),
        pl.BlockSpec((1, d), lambda i, f: (0, 0)),
        pl.BlockSpec((None, None, d, tf), lambda i, f: (layer, half, 0, chunk(f))),
        pl.BlockSpec((None, None, d, tf), lambda i, f: (layer, half, 0, chunk(f))),
        pl.BlockSpec((None, None, tf, d), lambda i, f: (layer, half, chunk(f), 0)),
    ]
    args = [h, gain.reshape(1, d), wg, wu, wd]
    if final:
        in_specs.append(pl.BlockSpec((1, d), lambda i, f: (0, 0)))
        args.append(final_gain.reshape(1, d))
    return pl.pallas_call(
        functools.partial(_ffn_kernel, nf=nf, tail=tail, tail_step=tail_step, dn=dn, final=final),
        grid=(m // tm, nf),
        in_specs=in_specs,
        out_specs=pl.BlockSpec((tm, d), lambda i, f: (i, 0)),
        out_shape=jax.ShapeDtypeStruct((m, d), F32),
        scratch_shapes=[pltpu.VMEM((tm, d), BF16)],
        compiler_params=_cparams(("parallel", "arbitrary")),
        name="ffn_final" if final else "ffn",
    )(*args)


def _qkv_kernel(h_ref, g_ref, w_ref, o_ref, n_ref, *, q_blocks, n_blocks, scale):
    @pl.when(pl.program_id(1) == 0)
    def _():
        n_ref[...] = _rms(h_ref[...], g_ref[...]).astype(BF16)

    col = _serpentine(pl.program_id(0), pl.program_id(1), n_blocks)
    acc = _dot(n_ref[...], w_ref[...])
    acc = acc * jnp.where(col < q_blocks, scale, 1.0).astype(F32)
    for c in range(o_ref.shape[0]):
        o_ref[c] = acc[:, c * HEAD_DIM:(c + 1) * HEAD_DIM].astype(BF16)


def _qkv(h, gain, w, *, tm=1024, tn=1024):
    m, d = h.shape
    n3 = w.shape[1]
    hpb = tn // HEAD_DIM
    nj = n3 // tn
    return pl.pallas_call(
        functools.partial(_qkv_kernel, q_blocks=(n3 // 3) // tn, n_blocks=nj, scale=HEAD_DIM ** -0.5 * LOG2E),
        grid=(m // tm, nj),
        in_specs=[
            pl.BlockSpec((tm, d), lambda i, j: (i, 0)),
            pl.BlockSpec((1, d), lambda i, j: (0, 0)),
            pl.BlockSpec((d, tn), lambda i, j: (0, _serpentine(i, j, nj))),
        ],
        out_specs=pl.BlockSpec((hpb, tm, HEAD_DIM), lambda i, j: (_serpentine(i, j, nj), i, 0)),
        out_shape=jax.ShapeDtypeStruct((n3 // HEAD_DIM, m, HEAD_DIM), BF16),
        scratch_shapes=[pltpu.VMEM((tm, d), BF16)],
        compiler_params=_cparams(("parallel", "arbitrary")),
        name="qkv",
    )(h, gain.reshape(1, d), w)


def _oproj_kernel(h_ref, *rest, n_in):
    a_refs = rest[:n_in]
    w_ref = rest[n_in]
    o_ref = rest[n_in + 1]
    acc = h_ref[...]
    off = 0
    for a_ref in a_refs:
        k = a_ref.shape[1]
        acc = acc + _dot(a_ref[...], w_ref[off:off + k, :])
        off += k
    o_ref[...] = acc


def _oproj(h, attn_parts, w, *, tm=512):
    m, d = h.shape
    in_specs = [pl.BlockSpec((tm, d), lambda i: (i, 0))]
    for a in attn_parts:
        in_specs.append(pl.BlockSpec((tm, a.shape[1]), lambda i: (i, 0)))
    in_specs.append(pl.BlockSpec(w.shape, lambda i: (0, 0)))
    return pl.pallas_call(
        functools.partial(_oproj_kernel, n_in=len(attn_parts)),
        grid=(m // tm,),
        in_specs=in_specs,
        out_specs=pl.BlockSpec((tm, d), lambda i: (i, 0)),
        out_shape=jax.ShapeDtypeStruct((m, d), F32),
        compiler_params=_cparams(("parallel",)),
        name="oproj",
    )(h, *attn_parts, w)


def _bucket_bias(dist, tab_ref, head):
    max_exact = REL_BUCKETS // 2
    d = jnp.maximum(dist, 0)
    df = jnp.maximum(d, 1).astype(F32)
    large = max_exact + (jnp.log(df / max_exact) / math.log(REL_MAX_DIST / max_exact)
                         * (REL_BUCKETS - max_exact)).astype(jnp.int32)
    large = jnp.minimum(large, REL_BUCKETS - 1)
    bucket = jnp.where(d < max_exact, d, large)
    out = jnp.full(dist.shape, tab_ref[0, head], F32)
    for b in range(1, REL_BUCKETS):
        out = jnp.where(bucket == b, tab_ref[b, head], out)
    return out


def _dilated_kernel(tab_ref, q_ref, k_ref, v_ref, o_ref, bias_ref, stage_ref, mid_ref, num_ref, lse_ref,
                    *, seq, head0):
    nbr = len(DILATIONS)
    h = pl.program_id(0)
    b = pl.program_id(1)
    w = DIL_W
    j = lax.broadcasted_iota(jnp.int32, (w, 2 * w), 1)

    @pl.when(b == 0)
    def _():
        i = lax.broadcasted_iota(jnp.int32, (w, 2 * w), 0)
        rel = w + i - j
        ok = (rel >= 0) & (rel <= w)
        for br, r in enumerate(DILATIONS):
            bias_ref[br] = jnp.where(ok, _bucket_bias(rel * r, tab_ref, head0 + h) * LOG2E, NEG)

    no_prev = jnp.where(j < w, NEG, 0.0).astype(F32)

    mid = DIL_MID
    for t, ref in enumerate((q_ref, k_ref, v_ref)):
        stage_ref[t] = ref[0, 0].astype(F32)
        for c in range(mid):
            mid_ref[t, c] = stage_ref[t, pl.ds(c, seq // mid, stride=mid), :]

    def subsequence(t, r, c):
        if r == mid:
            return mid_ref[t, c].astype(BF16)
        return mid_ref[t, c % mid, pl.ds(c // mid, seq // r, stride=r // mid), :].astype(BF16)

    for br, r in enumerate(DILATIONS):
        sub = seq // r
        nb = sub // w
        for c in range(r):
            rows = pl.ds(c, sub, stride=r) if r > 1 else slice(None)
            if r > 1:
                qc, kc, vc = (subsequence(t, r, c) for t in range(3))
            else:
                qc, kc, vc = q_ref[0, 0], k_ref[0, 0], v_ref[0, 0]

            def band(t):
                prev = jnp.concatenate([t[:w], t[:sub - w]], axis=0) if nb > 1 else t
                return jnp.concatenate([prev.reshape(nb, w, HEAD_DIM), t.reshape(nb, w, HEAD_DIM)], axis=1)

            s = lax.dot_general(qc.reshape(nb, w, HEAD_DIM), band(kc), (((2,), (2,)), ((0,), (0,))),
                                preferred_element_type=F32) + bias_ref[br][None]
            s = jnp.concatenate([s[:1] + no_prev[None], s[1:]], axis=0) if nb > 1 else s + no_prev[None]
            m = jnp.max(s, axis=-1, keepdims=True)
            p = jnp.exp2(s - m)
            den = jnp.sum(p, axis=-1, keepdims=True)
            o = lax.dot_general(p.astype(BF16), band(vc), (((2,), (1,)), ((0,), (0,))),
                                preferred_element_type=F32) / den
            lse = m + jnp.log(den) * LOG2E
            num_ref[br, rows, :] = o.reshape(sub, HEAD_DIM)
            lse_ref[br, rows, :] = jnp.broadcast_to(lse, (nb, w, HEAD_DIM)).reshape(sub, HEAD_DIM)

    ch = 512

    def combine(t, carry):
        rows = pl.ds(pl.multiple_of(t * ch, ch), ch)
        ls = [lse_ref[br, rows, :] for br in range(nbr)]
        m = functools.reduce(jnp.maximum, ls)
        ws = [jnp.exp2(l - m) for l in ls]
        tot = functools.reduce(lambda x, y: x + y, ws)
        acc = functools.reduce(lambda x, y: x + y, [wt * num_ref[br, rows, :] for br, wt in enumerate(ws)])
        o_ref[0, rows, :] = (acc / tot).astype(o_ref.dtype)
        return carry

    lax.fori_loop(0, seq // ch, combine, 0)


def _dilated(qkv, rel_bias, *, batch, seq, head0=0, n_heads=N_HEADS_A):
    nh_all = qkv.shape[0] // 3
    view = qkv.reshape(3 * nh_all, batch, seq, HEAD_DIM)
    nbr = len(DILATIONS)
    return pl.pallas_call(
        functools.partial(_dilated_kernel, seq=seq, head0=head0),
        grid=(n_heads, batch),
        in_specs=[pl.BlockSpec(memory_space=pltpu.SMEM)] + [
            pl.BlockSpec((1, 1, seq, HEAD_DIM), lambda h, b, j=j: (j * nh_all + head0 + h, b, 0, 0))
            for j in range(3)],
        out_specs=pl.BlockSpec((1, seq, HEAD_DIM), lambda h, b: (b, 0, h)),
        out_shape=jax.ShapeDtypeStruct((batch, seq, n_heads * HEAD_DIM), BF16),
        scratch_shapes=[
            pltpu.VMEM((nbr, DIL_W, 2 * DIL_W), F32),
            pltpu.VMEM((3, seq, HEAD_DIM), F32),
            pltpu.VMEM((3, DIL_MID, seq // DIL_MID, HEAD_DIM), F32),
            pltpu.VMEM((nbr, seq, HEAD_DIM), F32),
            pltpu.VMEM((nbr, seq, HEAD_DIM), F32),
        ],
        compiler_params=_cparams(("arbitrary", "arbitrary")),
        name="dilated",
    )(rel_bias, view, view, view)


def _moba_kernel(tab_ref, q_ref, k_ref, v_ref, o_ref, bias_ref, qa_ref, ka_ref, va_ref, *, seq, head0):
    blk = MOBA_BLOCK
    nblk = seq // blk
    hd = HEAD_DIM
    h = pl.program_id(0)
    b = pl.program_id(1)

    @pl.when(b == 0)
    def _():
        r = lax.broadcasted_iota(jnp.int32, (blk, blk), 0)
        c = lax.broadcasted_iota(jnp.int32, (blk, blk), 1)

        def fill(dlt, carry):
            dist = dlt * blk + r - c
            bias_ref[dlt] = jnp.where(dist >= 0, _bucket_bias(dist, tab_ref, head0 + h) * LOG2E, NEG)
            return carry

        lax.fori_loop(0, nblk, fill, 0)

    @pl.when((h == 0) & (b == 0))
    def _():
        key_blk = lax.broadcasted_iota(jnp.int32, (seq, hd), 0) // blk
        lane = lax.broadcasted_iota(jnp.int32, (seq, hd), 1)
        ka_ref[:, hd:] = jnp.where(lane == key_blk, MOBA_MASK, 0.0).astype(BF16)
        va_ref[:, hd:] = jnp.where(lane == 0, 1.0, 0.0).astype(BF16)

    q = q_ref[0, 0]
    k = k_ref[0, 0]
    qa_ref[:, :hd] = q
    ka_ref[:, :hd] = k
    va_ref[:, :hd] = v_ref[0, 0]

    kbar = jnp.sum(k.astype(F32).reshape(nblk, blk, hd), axis=1) * (1.0 / blk)
    k1 = kbar.astype(BF16)
    r1 = kbar - k1.astype(F32)
    k2 = r1.astype(BF16)
    k3 = (r1 - k2.astype(F32)).astype(BF16)
    gate = _dot_nt(k1, q) + _dot_nt(k2, q) + _dot_nt(k3, q)
    n_idx = lax.broadcasted_iota(jnp.int32, gate.shape, 0)
    own = lax.broadcasted_iota(jnp.int32, gate.shape, 1) // blk
    gate = jnp.where(n_idx < own, gate, -jnp.inf)
    skip = jnp.where(n_idx == own, 0.0, 1.0)
    for _ in range(min(MOBA_TOPK, nblk)):
        best = jnp.max(gate, axis=0, keepdims=True)
        idx = jnp.min(jnp.where(gate == best, n_idx, nblk), axis=0, keepdims=True)
        skip = jnp.where((n_idx == idx) & (best > -jnp.inf), 0.0, skip)
        gate = jnp.where(n_idx == idx, -jnp.inf, gate)
    skip = jnp.concatenate([skip, jnp.zeros((hd - nblk, seq), F32)], axis=0)
    qa_ref[:, hd:] = skip.T.astype(BF16)

    for i in range(nblk):
        width = (i + 1) * blk
        rows = slice(i * blk, (i + 1) * blk)
        s = _dot_nt(qa_ref[rows, :], ka_ref[:width, :])
        s = s + jnp.concatenate([bias_ref[i - n] for n in range(i + 1)], axis=1)
        m = jnp.max(s, axis=-1, keepdims=True)
        p = jnp.exp2(s - m).astype(BF16)
        o = _dot(p, va_ref[:width, :])
        o_ref[0, rows, :] = (o[:, :hd] / o[:, hd:hd + 1]).astype(o_ref.dtype)


def _moba(qkv, rel_bias, *, batch, seq, head0=N_HEADS_A, n_heads=N_HEADS - N_HEADS_A):
    nh_all = qkv.shape[0] // 3
    blk = MOBA_BLOCK
    nblk = seq // blk
    assert nblk <= HEAD_DIM
    view = qkv.reshape(3 * nh_all, batch, seq, HEAD_DIM)
    return pl.pallas_call(
        functools.partial(_moba_kernel, seq=seq, head0=head0),
        grid=(n_heads, batch),
        in_specs=[pl.BlockSpec(memory_space=pltpu.SMEM)] + [
            pl.BlockSpec((1, 1, seq, HEAD_DIM), lambda h, b, j=j: (j * nh_all + head0 + h, b, 0, 0))
            for j in range(3)],
        out_specs=pl.BlockSpec((1, seq, HEAD_DIM), lambda h, b: (b, 0, h)),
        out_shape=jax.ShapeDtypeStruct((batch, seq, n_heads * HEAD_DIM), BF16),
        scratch_shapes=[
            pltpu.VMEM((nblk, blk, blk), F32),
            pltpu.VMEM((seq, 2 * HEAD_DIM), BF16),
            pltpu.VMEM((seq, 2 * HEAD_DIM), BF16),
            pltpu.VMEM((seq, 2 * HEAD_DIM), BF16),
        ],
        compiler_params=_cparams(("arbitrary", "arbitrary")),
        name="moba",
    )(rel_bias, view, view, view)


def _sb_kernel(q_ref, k_ref, v_ref, o_ref, c_ref, acc_ref):
    t = SB_T
    i = pl.program_id(2)
    rows = lax.broadcasted_iota(jnp.int32, (t, t), 0)
    cols = lax.broadcasted_iota(jnp.int32, (t, t), 1)
    causal = cols < rows
    suffix = jnp.where(rows > cols, 1.0, 0.0).astype(BF16)
    sign_bit = jnp.uint32(0x80000000)

    def tile(q, krows, hh, keep):
        z = _dot_nt(q, k_ref[hh, 0, krows, :])
        neg_abs = lax.bitcast_convert_type(lax.bitcast_convert_type(z, jnp.uint32) | sign_bit, F32)
        sp = jnp.maximum(z, 0.0) + jnp.log(1.0 + jnp.exp2(neg_abs)) * LOG2E
        nk = sp if keep is None else jnp.where(keep, sp, 0.0)
        logw = (z - sp) - _dot(nk.astype(BF16), suffix)
        if keep is not None:
            logw = jnp.where(keep, logw, -jnp.inf)
        return nk, logw

    has_prev = i > 0
    own_rows = pl.ds(pl.multiple_of(i * t, t), t)
    prev_rows = pl.ds(pl.multiple_of(jnp.maximum(i - 1, 0) * t, t), t)
    for hh in range(SB_HEADS):
        q = q_ref[hh, 0]
        nk_a, lw_a = tile(q, own_rows, hh, causal)
        nk_b, lw_b = tile(q, prev_rows, hh, None)
        c_a = jnp.sum(nk_a, axis=-1, keepdims=True)
        p_a = jnp.exp2(lw_a).astype(BF16)
        p_b = jnp.exp2(jnp.where(has_prev, lw_b - c_a, -jnp.inf)).astype(BF16)
        acc_ref[hh] = _dot(p_a, v_ref[hh, 0, own_rows, :]) + _dot(p_b, v_ref[hh, 0, prev_rows, :])
        c_ref[hh] = c_a + jnp.where(has_prev, jnp.sum(nk_b, axis=-1, keepdims=True), 0.0)

    def cond(state):
        j, cmin = state
        return (j >= 0) & (cmin < SB_DEAD)

    def body(state):
        j, _ = state
        krows = pl.ds(pl.multiple_of(j * t, t), t)
        for hh in range(SB_HEADS):
            nk, lw = tile(q_ref[hh, 0], krows, hh, None)
            acc_ref[hh] += _dot(jnp.exp2(lw - c_ref[hh]).astype(BF16), v_ref[hh, 0, krows, :])
            c_ref[hh] += jnp.sum(nk, axis=-1, keepdims=True)
        return j - 1, jnp.min(c_ref[...])

    lax.while_loop(cond, body, (i - 2, jnp.min(c_ref[...])))
    o_ref[0] = jnp.concatenate([acc_ref[hh] for hh in range(SB_HEADS)], axis=1).astype(o_ref.dtype)


def _stickbreaking(qkv, *, batch, seq):
    nh = qkv.shape[0] // 3
    view = qkv.reshape(3 * nh, batch, seq, HEAD_DIM)
    g = SB_HEADS
    assert nh % g == 0
    return pl.pallas_call(
        _sb_kernel,
        grid=(batch, nh // g, seq // SB_T),
        in_specs=[
            pl.BlockSpec((g, 1, SB_T, HEAD_DIM), lambda b, h, i: (h, b, i, 0)),
            pl.BlockSpec((g, 1, seq, HEAD_DIM), lambda b, h, i: (nh // g + h, b, 0, 0)),
            pl.BlockSpec((g, 1, seq, HEAD_DIM), lambda b, h, i: (2 * (nh // g) + h, b, 0, 0)),
        ],
        out_specs=pl.BlockSpec((1, SB_T, g * HEAD_DIM), lambda b, h, i: (b, i, h)),
        out_shape=jax.ShapeDtypeStruct((batch, seq, nh * HEAD_DIM), BF16),
        scratch_shapes=[pltpu.VMEM((g, SB_T, 1), F32), pltpu.VMEM((g, SB_T, HEAD_DIM), F32)],
        compiler_params=_cparams(("parallel", "parallel", "arbitrary")),
        name="stickbreaking",
    )(view, view, view)


def kernel(x, ln_gains, ffn_w_gate, ffn_w_up, ffn_w_down, w_qkv_even, w_out_even, w_qkv_odd, w_out_odd,
           rel_bias, final_gain):
    batch, seq, d = x.shape
    depth = ln_gains.shape[0]
    m = batch * seq
    wg, wu, wd = (w.astype(BF16) for w in (ffn_w_gate, ffn_w_up, ffn_w_down))
    wqkv = (w_qkv_even.astype(BF16), w_qkv_odd.astype(BF16))
    wout = (w_out_even.astype(BF16), w_out_odd.astype(BF16))
    rel_bias = rel_bias.astype(F32)

    h = x.reshape(m, d)
    for i in range(depth):
        h = _ffn(h, ln_gains[i, 0], wg, wu, wd, i, 0)
        qkv = _qkv(h, ln_gains[i, 1], wqkv[i % 2][i // 2])
        if i % 2 == 0:
            parts = [_dilated(qkv, rel_bias, batch=batch, seq=seq), _moba(qkv, rel_bias, batch=batch, seq=seq)]
        else:
            parts = [_stickbreaking(qkv, batch=batch, seq=seq)]
        parts = [p.reshape(m, p.shape[-1]) for p in parts]
        h = _oproj(h, parts, wout[i % 2][i // 2])
        last = i == depth - 1
        h = _ffn(h, ln_gains[i, 2], wg, wu, wd, i, 1, final_gain if last else None)
    return h.reshape(batch, seq, d)
```

```python
import functools
import math

import jax
import jax.numpy as jnp
from jax import lax
from jax.experimental import pallas as pl
from jax.experimental.pallas import tpu as pltpu

F32 = jnp.float32
BF16 = jnp.bfloat16

HEAD_DIM = 128
N_HEADS = 16
N_HEADS_A = 8
DILATIONS = (1, 4, 16)
DIL_W = 128
DIL_MID = 4
MOBA_BLOCK = 256
MOBA_TOPK = 3
LOG2E = math.log2(math.e)
MOBA_MASK = -(2.0 ** 100)
REL_BUCKETS = 32
REL_MAX_DIST = 2048
FFN_HALF = 0.5
RMS_EPS = 1e-6
NEG = -1e30
SB_T = 256
SB_HEADS = 8
SB_DEAD = 152.0
V7X_VMEM_LIMIT = 56 * 1024 * 1024


def _cparams(sem):
    return pltpu.CompilerParams(dimension_semantics=sem, vmem_limit_bytes=V7X_VMEM_LIMIT)


def _rms(x, g):
    ms = jnp.mean(x * x, axis=-1, keepdims=True)
    return x * lax.rsqrt(ms + RMS_EPS) * g


def _dot(a, b):
    return jnp.dot(a, b, preferred_element_type=F32)


def _serpentine(outer, inner, n_inner):
    return jnp.where(outer % 2 == 0, inner, n_inner - 1 - inner)


def _dot_nt(a, b):
    return lax.dot_general(a, b, (((1,), (1,)), ((), ())), preferred_element_type=F32)


def _ffn_kernel(h_ref, g_ref, wg_ref, wu_ref, wd_ref, *rest, nf, tail, tail_step, dn, final):
    if final:
        fg_ref, o_ref, n_ref = rest
    else:
        o_ref, n_ref = rest
    f = pl.program_id(1)
    tf = wg_ref.shape[1]
    d = o_ref.shape[1]

    def chunk(width, first, last):
        if first:
            n = _rms(h_ref[...], g_ref[...]).astype(BF16)
            n_ref[...] = n
        else:
            n = n_ref[...]
        gate = _dot(n, wg_ref[:, :width])
        up = _dot(n, wu_ref[:, :width])
        a = (gate * (1.0 / (1.0 + jnp.exp(-gate))) * up).astype(BF16)
        for c in range(d // dn):
            sl = slice(c * dn, (c + 1) * dn)
            acc = _dot(a, wd_ref[:width, sl])
            if not first:
                acc = o_ref[:, sl] + acc
            o_ref[:, sl] = h_ref[:, sl] + FFN_HALF * acc if last else acc
        if last and final:
            o_ref[...] = _rms(o_ref[...], fg_ref[...])

    variants = {}
    for step in range(nf):
        key = (tail if step == tail_step else tf, step == 0, step == nf - 1)
        variants.setdefault(key, []).append(step)
    for (width, first, last), steps in variants.items():
        hit = functools.reduce(jnp.logical_or, [f == s for s in steps])
        pl.when(hit)(functools.partial(chunk, width, first, last))


def _ffn(h, gain, wg, wu, wd, layer, half, final_gain=None, *, tm=512, tf=1024, dn=512):
    m, d = h.shape
    dff = wg.shape[-1]
    nf = pl.cdiv(dff, tf)
    tail = dff - (nf - 1) * tf
    tail_step = (nf - 1) // 2
    final = final_gain is not None

    def chunk(f):
        return jnp.where(f == tail_step, nf - 1, jnp.where(f < tail_step, f, f - 1))

    in_specs = [
        pl.BlockSpec((tm, d), lambda i, f: (i, 0)),
        pl.BlockSpec((1, d), lambda i, f: (0, 0)),
        pl.BlockSpec((None, None, d, tf), lambda i, f: (layer, half, 0, chunk(f))),
        pl.BlockSpec((None, None, d, tf), lambda i, f: (layer, half, 0, chunk(f))),
        pl.BlockSpec((None, None, tf, d), lambda i, f: (layer, half, chunk(f), 0)),
    ]
    args = [h, gain.reshape(1, d), wg, wu, wd]
    if final:
        in_specs.append(pl.BlockSpec((1, d), lambda i, f: (0, 0)))
        args.append(final_gain.reshape(1, d))
    return pl.pallas_call(
        functools.partial(_ffn_kernel, nf=nf, tail=tail, tail_step=tail_step, dn=dn, final=final),
        grid=(m // tm, nf),
        in_specs=in_specs,
        out_specs=pl.BlockSpec((tm, d), lambda i, f: (i, 0)),
        out_shape=jax.ShapeDtypeStruct((m, d), F32),
        scratch_shapes=[pltpu.VMEM((tm, d), BF16)],
        compiler_params=_cparams(("parallel", "arbitrary")),
        name="ffn_final" if final else "ffn",
    )(*args)


def _qkv_kernel(h_ref, g_ref, w_ref, o_ref, n_ref, *, q_blocks, n_blocks, scale):
    @pl.when(pl.program_id(1) == 0)
    def _():
        n_ref[...] = _rms(h_ref[...], g_ref[...]).astype(BF16)

    col = _serpentine(pl.program_id(0), pl.program_id(1), n_blocks)
    acc = _dot(n_ref[...], w_ref[...])
    acc = acc * jnp.where(col < q_blocks, scale, 1.0).astype(F32)
    for c in range(o_ref.shape[0]):
        o_ref[c] = acc[:, c * HEAD_DIM:(c + 1) * HEAD_DIM].astype(BF16)


def _qkv(h, gain, w, *, tm=1024, tn=1024):
    m, d = h.shape
    n3 = w.shape[1]
    hpb = tn // HEAD_DIM
    nj = n3 // tn
    return pl.pallas_call(
        functools.partial(_qkv_kernel, q_blocks=(n3 // 3) // tn, n_blocks=nj, scale=HEAD_DIM ** -0.5 * LOG2E),
        grid=(m // tm, nj),
        in_specs=[
            pl.BlockSpec((tm, d), lambda i, j: (i, 0)),
            pl.BlockSpec((1, d), lambda i, j: (0, 0)),
            pl.BlockSpec((d, tn), lambda i, j: (0, _serpentine(i, j, nj))),
        ],
        out_specs=pl.BlockSpec((hpb, tm, HEAD_DIM), lambda i, j: (_serpentine(i, j, nj), i, 0)),
        out_shape=jax.ShapeDtypeStruct((n3 // HEAD_DIM, m, HEAD_DIM), BF16),
        scratch_shapes=[pltpu.VMEM((tm, d), BF16)],
        compiler_params=_cparams(("parallel", "arbitrary")),
        name="qkv",
    )(h, gain.reshape(1, d), w)


def _oproj_kernel(h_ref, *rest, n_in):
    a_refs = rest[:n_in]
    w_ref = rest[n_in]
    o_ref = rest[n_in + 1]
    acc = h_ref[...]
    off = 0
    for a_ref in a_refs:
        k = a_ref.shape[1]
        acc = acc + _dot(a_ref[...], w_ref[off:off + k, :])
        off += k
    o_ref[...] = acc


def _oproj(h, attn_parts, w, *, tm=512):
    m, d = h.shape
    in_specs = [pl.BlockSpec((tm, d), lambda i: (i, 0))]
    for a in attn_parts:
        in_specs.append(pl.BlockSpec((tm, a.shape[1]), lambda i: (i, 0)))
    in_specs.append(pl.BlockSpec(w.shape, lambda i: (0, 0)))
    return pl.pallas_call(
        functools.partial(_oproj_kernel, n_in=len(attn_parts)),
        grid=(m // tm,),
        in_specs=in_specs,
        out_specs=pl.BlockSpec((tm, d), lambda i: (i, 0)),
        out_shape=jax.ShapeDtypeStruct((m, d), F32),
        compiler_params=_cparams(("parallel",)),
        name="oproj",
    )(h, *attn_parts, w)


def _bucket_bias(dist, tab_ref, head):
    max_exact = REL_BUCKETS // 2
    d = jnp.maximum(dist, 0)
    df = jnp.maximum(d, 1).astype(F32)
    large = max_exact + (jnp.log(df / max_exact) / math.log(REL_MAX_DIST / max_exact)
                         * (REL_BUCKETS - max_exact)).astype(jnp.int32)
    large = jnp.minimum(large, REL_BUCKETS - 1)
    bucket = jnp.where(d < max_exact, d, large)
    out = jnp.full(dist.shape, tab_ref[0, head], F32)
    for b in range(1, REL_BUCKETS):
        out = jnp.where(bucket == b, tab_ref[b, head], out)
    return out


def _dilated_kernel(tab_ref, q_ref, k_ref, v_ref, o_ref, bias_ref, stage_ref, mid_ref, num_ref, lse_ref,
                    *, seq, head0):
    nbr = len(DILATIONS)
    h = pl.program_id(0)
    b = pl.program_id(1)
    w = DIL_W
    j = lax.broadcasted_iota(jnp.int32, (w, 2 * w), 1)

    @pl.when(b == 0)
    def _():
        i = lax.broadcasted_iota(jnp.int32, (w, 2 * w), 0)
        rel = w + i - j
        ok = (rel >= 0) & (rel <= w)
        for br, r in enumerate(DILATIONS):
            bias_ref[br] = jnp.where(ok, _bucket_bias(rel * r, tab_ref, head0 + h) * LOG2E, NEG)

    no_prev = jnp.where(j < w, NEG, 0.0).astype(F32)

    mid = DIL_MID
    for t, ref in enumerate((q_ref, k_ref, v_ref)):
        stage_ref[t] = ref[0, 0].astype(F32)
        for c in range(mid):
            mid_ref[t, c] = stage_ref[t, pl.ds(c, seq // mid, stride=mid), :]

    def subsequence(t, r, c):
        if r == mid:
            return mid_ref[t, c].astype(BF16)
        return mid_ref[t, c % mid, pl.ds(c // mid, seq // r, stride=r // mid), :].astype(BF16)

    for br, r in enumerate(DILATIONS):
        sub = seq // r
        nb = sub // w
        for c in range(r):
            rows = pl.ds(c, sub, stride=r) if r > 1 else slice(None)
            if r > 1:
                qc, kc, vc = (subsequence(t, r, c) for t in range(3))
            else:
                qc, kc, vc = q_ref[0, 0], k_ref[0, 0], v_ref[0, 0]

            def band(t):
                prev = jnp.concatenate([t[:w], t[:sub - w]], axis=0) if nb > 1 else t
                return jnp.concatenate([prev.reshape(nb, w, HEAD_DIM), t.reshape(nb, w, HEAD_DIM)], axis=1)

            s = lax.dot_general(qc.reshape(nb, w, HEAD_DIM), band(kc), (((2,), (2,)), ((0,), (0,))),
                                preferred_element_type=F32) + bias_ref[br][None]
            s = jnp.concatenate([s[:1] + no_prev[None], s[1:]], axis=0) if nb > 1 else s + no_prev[None]
            m = jnp.max(s, axis=-1, keepdims=True)
            p = jnp.exp2(s - m)
            den = jnp.sum(p, axis=-1, keepdims=True)
            o = lax.dot_general(p.astype(BF16), band(vc), (((2,), (1,)), ((0,), (0,))),
                                preferred_element_type=F32) / den
            lse = m + jnp.log(den) * LOG2E
            num_ref[br, rows, :] = o.reshape(sub, HEAD_DIM)
            lse_ref[br, rows, :] = jnp.broadcast_to(lse, (nb, w, HEAD_DIM)).reshape(sub, HEAD_DIM)

    ch = 512

    def combine(t, carry):
        rows = pl.ds(pl.multiple_of(t * ch, ch), ch)
        ls = [lse_ref[br, rows, :] for br in range(nbr)]
        m = functools.reduce(jnp.maximum, ls)
        ws = [jnp.exp2(l - m) for l in ls]
        tot = functools.reduce(lambda x, y: x + y, ws)
        acc = functools.reduce(lambda x, y: x + y, [wt * num_ref[br, rows, :] for br, wt in enumerate(ws)])
        o_ref[0, rows, :] = (acc / tot).astype(o_ref.dtype)
        return carry

    lax.fori_loop(0, seq // ch, combine, 0)


def _dilated(qkv, rel_bias, *, batch, seq, head0=0, n_heads=N_HEADS_A):
    nh_all = qkv.shape[0] // 3
    view = qkv.reshape(3 * nh_all, batch, seq, HEAD_DIM)
    nbr = len(DILATIONS)
    return pl.pallas_call(
        functools.partial(_dilated_kernel, seq=seq, head0=head0),
        grid=(n_heads, batch),
        in_specs=[pl.BlockSpec(memory_space=pltpu.SMEM)] + [
            pl.BlockSpec((1, 1, seq, HEAD_DIM), lambda h, b, j=j: (j * nh_all + head0 + h, b, 0, 0))
            for j in range(3)],
        out_specs=pl.BlockSpec((1, seq, HEAD_DIM), lambda h, b: (b, 0, h)),
        out_shape=jax.ShapeDtypeStruct((batch, seq, n_heads * HEAD_DIM), BF16),
        scratch_shapes=[
            pltpu.VMEM((nbr, DIL_W, 2 * DIL_W), F32),
            pltpu.VMEM((3, seq, HEAD_DIM), F32),
            pltpu.VMEM((3, DIL_MID, seq // DIL_MID, HEAD_DIM), F32),
            pltpu.VMEM((nbr, seq, HEAD_DIM), F32),
            pltpu.VMEM((nbr, seq, HEAD_DIM), F32),
        ],
        compiler_params=_cparams(("arbitrary", "arbitrary")),
        name="dilated",
    )(rel_bias, view, view, view)


def _moba_kernel(tab_ref, q_ref, k_ref, v_ref, o_ref, bias_ref, qa_ref, ka_ref, va_ref, *, seq, head0):
    blk = MOBA_BLOCK
    nblk = seq // blk
    hd = HEAD_DIM
    h = pl.program_id(0)
    b = pl.program_id(1)

    @pl.when(b == 0)
    def _():
        r = lax.broadcasted_iota(jnp.int32, (blk, blk), 0)
        c = lax.broadcasted_iota(jnp.int32, (blk, blk), 1)

        def fill(dlt, carry):
            dist = dlt * blk + r - c
            bias_ref[dlt] = jnp.where(dist >= 0, _bucket_bias(dist, tab_ref, head0 + h) * LOG2E, NEG)
            return carry

        lax.fori_loop(0, nblk, fill, 0)

    @pl.when((h == 0) & (b == 0))
    def _():
        key_blk = lax.broadcasted_iota(jnp.int32, (seq, hd), 0) // blk
        lane = lax.broadcasted_iota(jnp.int32, (seq, hd), 1)
        ka_ref[:, hd:] = jnp.where(lane == key_blk, MOBA_MASK, 0.0).astype(BF16)
        va_ref[:, hd:] = jnp.where(lane == 0, 1.0, 0.0).astype(BF16)

    q = q_ref[0, 0]
    k = k_ref[0, 0]
    qa_ref[:, :hd] = q
    ka_ref[:, :hd] = k
    va_ref[:, :hd] = v_ref[0, 0]

    kbar = jnp.sum(k.astype(F32).reshape(nblk, blk, hd), axis=1) * (1.0 / blk)
    k1 = kbar.astype(BF16)
    r1 = kbar - k1.astype(F32)
    k2 = r1.astype(BF16)
    k3 = (r1 - k2.astype(F32)).astype(BF16)
    gate = _dot_nt(k1, q) + _dot_nt(k2, q) + _dot_nt(k3, q)
    n_idx = lax.broadcasted_iota(jnp.int32, gate.shape, 0)
    own = lax.broadcasted_iota(jnp.int32, gate.shape, 1) // blk
    gate = jnp.where(n_idx < own, gate, -jnp.inf)
    skip = jnp.where(n_idx == own, 0.0, 1.0)
    for _ in range(min(MOBA_TOPK, nblk)):
        best = jnp.max(gate, axis=0, keepdims=True)
        idx = jnp.min(jnp.where(gate == best, n_idx, nblk), axis=0, keepdims=True)
        skip = jnp.where((n_idx == idx) & (best > -jnp.inf), 0.0, skip)
        gate = jnp.where(n_idx == idx, -jnp.inf, gate)
    skip = jnp.concatenate([skip, jnp.zeros((hd - nblk, seq), F32)], axis=0)
    qa_ref[:, hd:] = skip.T.astype(BF16)

    for i in range(nblk):
        width = (i + 1) * blk
        rows = slice(i * blk, (i + 1) * blk)
        s = _dot_nt(qa_ref[rows, :], ka_ref[:width, :])
        s = s + jnp.concatenate([bias_ref[i - n] for n in range(i + 1)], axis=1)
        m = jnp.max(s, axis=-1, keepdims=True)
        p = jnp.exp2(s - m).astype(BF16)
        o = _dot(p, va_ref[:width, :])
        o_ref[0, rows, :] = (o[:, :hd] / o[:, hd:hd + 1]).astype(o_ref.dtype)


def _moba(qkv, rel_bias, *, batch, seq, head0=N_HEADS_A, n_heads=N_HEADS - N_HEADS_A):
    nh_all = qkv.shape[0] // 3
    blk = MOBA_BLOCK
    nblk = seq // blk
    assert nblk <= HEAD_DIM
    view = qkv.reshape(3 * nh_all, batch, seq, HEAD_DIM)
    return pl.pallas_call(
        functools.partial(_moba_kernel, seq=seq, head0=head0),
        grid=(n_heads, batch),
        in_specs=[pl.BlockSpec(memory_space=pltpu.SMEM)] + [
            pl.BlockSpec((1, 1, seq, HEAD_DIM), lambda h, b, j=j: (j * nh_all + head0 + h, b, 0, 0))
            for j in range(3)],
        out_specs=pl.BlockSpec((1, seq, HEAD_DIM), lambda h, b: (b, 0, h)),
        out_shape=jax.ShapeDtypeStruct((batch, seq, n_heads * HEAD_DIM), BF16),
        scratch_shapes=[
            pltpu.VMEM((nblk, blk, blk), F32),
            pltpu.VMEM((seq, 2 * HEAD_DIM), BF16),
            pltpu.VMEM((seq, 2 * HEAD_DIM), BF16),
            pltpu.VMEM((seq, 2 * HEAD_DIM), BF16),
        ],
        compiler_params=_cparams(("arbitrary", "arbitrary")),
        name="moba",
    )(rel_bias, view, view, view)


def _sb_kernel(q_ref, k_ref, v_ref, o_ref, c_ref, acc_ref):
    t = SB_T
    i = pl.program_id(2)
    rows = lax.broadcasted_iota(jnp.int32, (t, t), 0)
    cols = lax.broadcasted_iota(jnp.int32, (t, t), 1)
    causal = cols < rows
    suffix = jnp.where(rows > cols, 1.0, 0.0).astype(BF16)
    sign_bit = jnp.uint32(0x80000000)

    def tile(q, krows, hh, keep):
        z = _dot_nt(q, k_ref[hh, 0, krows, :])
        neg_abs = lax.bitcast_convert_type(lax.bitcast_convert_type(z, jnp.uint32) | sign_bit, F32)
        sp = jnp.maximum(z, 0.0) + jnp.log(1.0 + jnp.exp2(neg_abs)) * LOG2E
        nk = sp if keep is None else jnp.where(keep, sp, 0.0)
        logw = (z - sp) - _dot(nk.astype(BF16), suffix)
        if keep is not None:
            logw = jnp.where(keep, logw, -jnp.inf)
        return nk, logw

    has_prev = i > 0
    own_rows = pl.ds(pl.multiple_of(i * t, t), t)
    prev_rows = pl.ds(pl.multiple_of(jnp.maximum(i - 1, 0) * t, t), t)
    for hh in range(SB_HEADS):
        q = q_ref[hh, 0]
        nk_a, lw_a = tile(q, own_rows, hh, causal)
        nk_b, lw_b = tile(q, prev_rows, hh, None)
        c_a = jnp.sum(nk_a, axis=-1, keepdims=True)
        p_a = jnp.exp2(lw_a).astype(BF16)
        p_b = jnp.exp2(jnp.where(has_prev, lw_b - c_a, -jnp.inf)).astype(BF16)
        acc_ref[hh] = _dot(p_a, v_ref[hh, 0, own_rows, :]) + _dot(p_b, v_ref[hh, 0, prev_rows, :])
        c_ref[hh] = c_a + jnp.where(has_prev, jnp.sum(nk_b, axis=-1, keepdims=True), 0.0)

    def cond(state):
        j, cmin = state
        return (j >= 0) & (cmin < SB_DEAD)

    def body(state):
        j, _ = state
        krows = pl.ds(pl.multiple_of(j * t, t), t)
        for hh in range(SB_HEADS):
            nk, lw = tile(q_ref[hh, 0], krows, hh, None)
            acc_ref[hh] += _dot(jnp.exp2(lw - c_ref[hh]).astype(BF16), v_ref[hh, 0, krows, :])
            c_ref[hh] += jnp.sum(nk, axis=-1, keepdims=True)
        return j - 1, jnp.min(c_ref[...])

    lax.while_loop(cond, body, (i - 2, jnp.min(c_ref[...])))
    o_ref[0] = jnp.concatenate([acc_ref[hh] for hh in range(SB_HEADS)], axis=1).astype(o_ref.dtype)


def _stickbreaking(qkv, *, batch, seq):
    nh = qkv.shape[0] // 3
    view = qkv.reshape(3 * nh, batch, seq, HEAD_DIM)
    g = SB_HEADS
    assert nh % g == 0
    return pl.pallas_call(
        _sb_kernel,
        grid=(batch, nh // g, seq // SB_T),
        in_specs=[
            pl.BlockSpec((g, 1, SB_T, HEAD_DIM), lambda b, h, i: (h, b, i, 0)),
            pl.BlockSpec((g, 1, seq, HEAD_DIM), lambda b, h, i: (nh // g + h, b, 0, 0)),
            pl.BlockSpec((g, 1, seq, HEAD_DIM), lambda b, h, i: (2 * (nh // g) + h, b, 0, 0)),
        ],
        out_specs=pl.BlockSpec((1, SB_T, g * HEAD_DIM), lambda b, h, i: (b, i, h)),
        out_shape=jax.ShapeDtypeStruct((batch, seq, nh * HEAD_DIM), BF16),
        scratch_shapes=[pltpu.VMEM((g, SB_T, 1), F32), pltpu.VMEM((g, SB_T, HEAD_DIM), F32)],
        compiler_params=_cparams(("parallel", "parallel", "arbitrary")),
        name="stickbreaking",
    )(view, view, view)


def kernel(x, ln_gains, ffn_w_gate, ffn_w_up, ffn_w_down, w_qkv_even, w_out_even, w_qkv_odd, w_out_odd,
           rel_bias, final_gain):
    batch, seq, d = x.shape
    depth = ln_gains.shape[0]
    m = batch * seq
    wg, wu, wd = (w.astype(BF16) for w in (ffn_w_gate, ffn_w_up, ffn_w_down))
    wqkv = (w_qkv_even.astype(BF16), w_qkv_odd.astype(BF16))
    wout = (w_out_even.astype(BF16), w_out_odd.astype(BF16))
    rel_bias = rel_bias.astype(F32)

    h = x.reshape(m, d)
    for i in range(depth):
        h = _ffn(h, ln_gains[i, 0], wg, wu, wd, i, 0)
        qkv = _qkv(h, ln_gains[i, 1], wqkv[i % 2][i // 2])
        if i % 2 == 0:
            parts = [_dilated(qkv, rel_bias, batch=batch, seq=seq), _moba(qkv, rel_bias, batch=batch, seq=seq)]
        else:
            parts = [_stickbreaking(qkv, batch=batch, seq=seq)]
        parts = [p.reshape(m, p.shape[-1]) for p in parts]
        h = _oproj(h, parts, wout[i % 2][i // 2])
        last = i == depth - 1
        h = _ffn(h, ln_gains[i, 2], wg, wu, wd, i, 1, final_gain if last else None)
    return h.reshape(batch, seq, d)
```

```python
import functools
import math

import jax
import jax.numpy as jnp
from jax import lax
from jax.experimental import pallas as pl
from jax.experimental.pallas import tpu as pltpu

F32 = jnp.float32
BF16 = jnp.bfloat16

HEAD_DIM = 128
N_HEADS = 16
N_HEADS_A = 8
DILATIONS = (1, 4, 16)
DIL_W = 128
DIL_MID = 4
MOBA_BLOCK = 256
MOBA_TOPK = 3
LOG2E = math.log2(math.e)
MOBA_MASK = -(2.0 ** 100)
REL_BUCKETS = 32
REL_MAX_DIST = 2048
FFN_HALF = 0.5
RMS_EPS = 1e-6
NEG = -1e30
SB_T = 256
SB_HEADS = 8
SB_DEAD = 152.0
V7X_VMEM_LIMIT = 56 * 1024 * 1024


def _cparams(sem):
    return pltpu.CompilerParams(dimension_semantics=sem, vmem_limit_bytes=V7X_VMEM_LIMIT)


def _rms(x, g):
    ms = jnp.mean(x * x, axis=-1, keepdims=True)
    return x * lax.rsqrt(ms + RMS_EPS) * g


def _dot(a, b):
    return jnp.dot(a, b, preferred_element_type=F32)


def _serpentine(outer, inner, n_inner):
    return jnp.where(outer % 2 == 0, inner, n_inner - 1 - inner)


def _dot_nt(a, b):
    return lax.dot_general(a, b, (((1,), (1,)), ((), ())), preferred_element_type=F32)


def _ffn_kernel(h_ref, g_ref, wg_ref, wu_ref, wd_ref, *rest, nf, tail, tail_step, dn, final):
    if final:
        fg_ref, o_ref, n_ref = rest
    else:
        o_ref, n_ref = rest
    f = pl.program_id(1)
    tf = wg_ref.shape[1]
    d = o_ref.shape[1]

    def chunk(width, first, last):
        if first:
            n = _rms(h_ref[...], g_ref[...]).astype(BF16)
            n_ref[...] = n
        else:
            n = n_ref[...]
        gate = _dot(n, wg_ref[:, :width])
        up = _dot(n, wu_ref[:, :width])
        a = (gate * (1.0 / (1.0 + jnp.exp(-gate))) * up).astype(BF16)
        for c in range(d // dn):
            sl = slice(c * dn, (c + 1) * dn)
            acc = _dot(a, wd_ref[:width, sl])
            if not first:
                acc = o_ref[:, sl] + acc
            o_ref[:, sl] = h_ref[:, sl] + FFN_HALF * acc if last else acc
        if last and final:
            o_ref[...] = _rms(o_ref[...], fg_ref[...])

    variants = {}
    for step in range(nf):
        key = (tail if step == tail_step else tf, step == 0, step == nf - 1)
        variants.setdefault(key, []).append(step)
    for (width, first, last), steps in variants.items():
        hit = functools.reduce(jnp.logical_or, [f == s for s in steps])
        pl.when(hit)(functools.partial(chunk, width, first, last))


def _ffn(h, gain, wg, wu, wd, layer, half, final_gain=None, *, tm=512, tf=1024, dn=512):
    m, d = h.shape
    dff = wg.shape[-1]
    nf = pl.cdiv(dff, tf)
    tail = dff - (nf - 1) * tf
    tail_step = (nf - 1) // 2
    final = final_gain is not None

    def chunk(f):
        return jnp.where(f == tail_step, nf - 1, jnp.where(f < tail_step, f, f - 1))

    in_specs = [
        pl.BlockSpec((tm, d), lambda i, f: (i, 0)),
        pl.BlockSpec((1, d), lambda i, f: (0, 0)),
        pl.BlockSpec((None, None, d, tf), lambda i, f: (layer, half, 0, chunk(f))),
        pl.BlockSpec((None, None, d, tf), lambda i, f: (layer, half, 0, chunk(f))),
        pl.BlockSpec((None, None, tf, d), lambda i, f: (layer, half, chunk(f), 0)),
    ]
    args = [h, gain.reshape(1, d), wg, wu, wd]
    if final:
        in_specs.append(pl.BlockSpec((1, d), lambda i, f: (0, 0)))
        args.append(final_gain.reshape(1, d))
    return pl.pallas_call(
        functools.partial(_ffn_kernel, nf=nf, tail=tail, tail_step=tail_step, dn=dn, final=final),
        grid=(m // tm, nf),
        in_specs=in_specs,
        out_specs=pl.BlockSpec((tm, d), lambda i, f: (i, 0)),
        out_shape=jax.ShapeDtypeStruct((m, d), F32),
        scratch_shapes=[pltpu.VMEM((tm, d), BF16)],
        compiler_params=_cparams(("parallel", "arbitrary")),
        name="ffn_final" if final else "ffn",
    )(*args)


def _qkv_kernel(h_ref, g_ref, w_ref, o_ref, n_ref, *, q_blocks, n_blocks, scale):
    col = _serpentine(pl.program_id(0), pl.program_id(1), n_blocks)

    def project(first):
        if first:
            n = _rms(h_ref[...], g_ref[...]).astype(BF16)
            n_ref[...] = n
        else:
            n = n_ref[...]
        acc = _dot(n, w_ref[...])
        acc = acc * jnp.where(col < q_blocks, scale, 1.0).astype(F32)
        for c in range(o_ref.shape[0]):
            o_ref[c] = acc[:, c * HEAD_DIM:(c + 1) * HEAD_DIM].astype(BF16)

    pl.when(pl.program_id(1) == 0)(functools.partial(project, True))
    pl.when(pl.program_id(1) > 0)(functools.partial(project, False))


def _qkv(h, gain, w, *, tm=1024, tn=1024):
    m, d = h.shape
    n3 = w.shape[1]
    hpb = tn // HEAD_DIM
    nj = n3 // tn
    return pl.pallas_call(
        functools.partial(_qkv_kernel, q_blocks=(n3 // 3) // tn, n_blocks=nj, scale=HEAD_DIM ** -0.5 * LOG2E),
        grid=(m // tm, nj),
        in_specs=[
            pl.BlockSpec((tm, d), lambda i, j: (i, 0)),
            pl.BlockSpec((1, d), lambda i, j: (0, 0)),
            pl.BlockSpec((d, tn), lambda i, j: (0, _serpentine(i, j, nj))),
        ],
        out_specs=pl.BlockSpec((hpb, tm, HEAD_DIM), lambda i, j: (_serpentine(i, j, nj), i, 0)),
        out_shape=jax.ShapeDtypeStruct((n3 // HEAD_DIM, m, HEAD_DIM), BF16),
        scratch_shapes=[pltpu.VMEM((tm, d), BF16)],
        compiler_params=_cparams(("parallel", "arbitrary")),
        name="qkv",
    )(h, gain.reshape(1, d), w)


def _oproj_kernel(h_ref, *rest, n_in):
    a_refs = rest[:n_in]
    w_ref = rest[n_in]
    o_ref = rest[n_in + 1]
    acc = h_ref[...]
    off = 0
    for a_ref in a_refs:
        k = a_ref.shape[1]
        acc = acc + _dot(a_ref[...], w_ref[off:off + k, :])
        off += k
    o_ref[...] = acc


def _oproj(h, attn_parts, w, *, tm=512):
    m, d = h.shape
    in_specs = [pl.BlockSpec((tm, d), lambda i: (i, 0))]
    for a in attn_parts:
        in_specs.append(pl.BlockSpec((tm, a.shape[1]), lambda i: (i, 0)))
    in_specs.append(pl.BlockSpec(w.shape, lambda i: (0, 0)))
    return pl.pallas_call(
        functools.partial(_oproj_kernel, n_in=len(attn_parts)),
        grid=(m // tm,),
        in_specs=in_specs,
        out_specs=pl.BlockSpec((tm, d), lambda i: (i, 0)),
        out_shape=jax.ShapeDtypeStruct((m, d), F32),
        compiler_params=_cparams(("parallel",)),
        name="oproj",
    )(h, *attn_parts, w)


def _bucket_bias(dist, tab_ref, head):
    max_exact = REL_BUCKETS // 2
    d = jnp.maximum(dist, 0)
    df = jnp.maximum(d, 1).astype(F32)
    large = max_exact + (jnp.log(df / max_exact) / math.log(REL_MAX_DIST / max_exact)
                         * (REL_BUCKETS - max_exact)).astype(jnp.int32)
    large = jnp.minimum(large, REL_BUCKETS - 1)
    bucket = jnp.where(d < max_exact, d, large)
    out = jnp.full(dist.shape, tab_ref[0, head], F32)
    for b in range(1, REL_BUCKETS):
        out = jnp.where(bucket == b, tab_ref[b, head], out)
    return out


def _dilated_kernel(tab_ref, q_ref, k_ref, v_ref, o_ref, bias_ref, stage_ref, mid_ref, num_ref, lse_ref,
                    *, seq, head0):
    nbr = len(DILATIONS)
    h = pl.program_id(0)
    b = pl.program_id(1)
    w = DIL_W
    j = lax.broadcasted_iota(jnp.int32, (w, 2 * w), 1)

    @pl.when(b == 0)
    def _():
        i = lax.broadcasted_iota(jnp.int32, (w, 2 * w), 0)
        rel = w + i - j
        ok = (rel >= 0) & (rel <= w)
        for br, r in enumerate(DILATIONS):
            bias_ref[br] = jnp.where(ok, _bucket_bias(rel * r, tab_ref, head0 + h) * LOG2E, NEG)

    no_prev = jnp.where(j < w, NEG, 0.0).astype(F32)

    mid = DIL_MID
    for t, ref in enumerate((q_ref, k_ref, v_ref)):
        stage_ref[t] = ref[0, 0].astype(F32)
        for c in range(mid):
            mid_ref[t, c] = stage_ref[t, pl.ds(c, seq // mid, stride=mid), :]

    def subsequence(t, r, c):
        if r == mid:
            return mid_ref[t, c].astype(BF16)
        return mid_ref[t, c % mid, pl.ds(c // mid, seq // r, stride=r // mid), :].astype(BF16)

    for br, r in enumerate(DILATIONS):
        sub = seq // r
        nb = sub // w
        for c in range(r):
            rows = pl.ds(c, sub, stride=r) if r > 1 else slice(None)
            if r > 1:
                qc, kc, vc = (subsequence(t, r, c) for t in range(3))
            else:
                qc, kc, vc = q_ref[0, 0], k_ref[0, 0], v_ref[0, 0]

            def band(t):
                prev = jnp.concatenate([t[:w], t[:sub - w]], axis=0) if nb > 1 else t
                return jnp.concatenate([prev.reshape(nb, w, HEAD_DIM), t.reshape(nb, w, HEAD_DIM)], axis=1)

            s = lax.dot_general(qc.reshape(nb, w, HEAD_DIM), band(kc), (((2,), (2,)), ((0,), (0,))),
                                preferred_element_type=F32) + bias_ref[br][None]
            s = jnp.concatenate([s[:1] + no_prev[None], s[1:]], axis=0) if nb > 1 else s + no_prev[None]
            m = jnp.max(s, axis=-1, keepdims=True)
            p = jnp.exp2(s - m)
            den = jnp.sum(p, axis=-1, keepdims=True)
            o = lax.dot_general(p.astype(BF16), band(vc), (((2,), (1,)), ((0,), (0,))),
                                preferred_element_type=F32) / den
            lse = m + jnp.log(den) * LOG2E
            num_ref[br, rows, :] = o.reshape(sub, HEAD_DIM)
            lse_ref[br, rows, :] = jnp.broadcast_to(lse, (nb, w, HEAD_DIM)).reshape(sub, HEAD_DIM)

    ch = 512

    def combine(t, carry):
        rows = pl.ds(pl.multiple_of(t * ch, ch), ch)
        ls = [lse_ref[br, rows, :] for br in range(nbr)]
        m = functools.reduce(jnp.maximum, ls)
        ws = [jnp.exp2(l - m) for l in ls]
        tot = functools.reduce(lambda x, y: x + y, ws)
        acc = functools.reduce(lambda x, y: x + y, [wt * num_ref[br, rows, :] for br, wt in enumerate(ws)])
        o_ref[0, rows, :] = (acc / tot).astype(o_ref.dtype)
        return carry

    lax.fori_loop(0, seq // ch, combine, 0)


def _dilated(qkv, rel_bias, *, batch, seq, head0=0, n_heads=N_HEADS_A):
    nh_all = qkv.shape[0] // 3
    view = qkv.reshape(3 * nh_all, batch, seq, HEAD_DIM)
    nbr = len(DILATIONS)
    return pl.pallas_call(
        functools.partial(_dilated_kernel, seq=seq, head0=head0),
        grid=(n_heads, batch),
        in_specs=[pl.BlockSpec(memory_space=pltpu.SMEM)] + [
            pl.BlockSpec((1, 1, seq, HEAD_DIM), lambda h, b, j=j: (j * nh_all + head0 + h, b, 0, 0))
            for j in range(3)],
        out_specs=pl.BlockSpec((1, seq, HEAD_DIM), lambda h, b: (b, 0, h)),
        out_shape=jax.ShapeDtypeStruct((batch, seq, n_heads * HEAD_DIM), BF16),
        scratch_shapes=[
            pltpu.VMEM((nbr, DIL_W, 2 * DIL_W), F32),
            pltpu.VMEM((3, seq, HEAD_DIM), F32),
            pltpu.VMEM((3, DIL_MID, seq // DIL_MID, HEAD_DIM), F32),
            pltpu.VMEM((nbr, seq, HEAD_DIM), F32),
            pltpu.VMEM((nbr, seq, HEAD_DIM), F32),
        ],
        compiler_params=_cparams(("arbitrary", "arbitrary")),
        name="dilated",
    )(rel_bias, view, view, view)


def _moba_kernel(tab_ref, q_ref, k_ref, v_ref, o_ref, bias_ref, qa_ref, ka_ref, va_ref, *, seq, head0):
    blk = MOBA_BLOCK
    nblk = seq // blk
    hd = HEAD_DIM
    h = pl.program_id(0)
    b = pl.program_id(1)

    @pl.when(b == 0)
    def _():
        r = lax.broadcasted_iota(jnp.int32, (blk, blk), 0)
        c = lax.broadcasted_iota(jnp.int32, (blk, blk), 1)

        def fill(dlt, carry):
            dist = dlt * blk + r - c
            bias_ref[dlt] = jnp.where(dist >= 0, _bucket_bias(dist, tab_ref, head0 + h) * LOG2E, NEG)
            return carry

        lax.fori_loop(0, nblk, fill, 0)

    @pl.when((h == 0) & (b == 0))
    def _():
        key_blk = lax.broadcasted_iota(jnp.int32, (seq, hd), 0) // blk
        lane = lax.broadcasted_iota(jnp.int32, (seq, hd), 1)
        ka_ref[:, hd:] = jnp.where(lane == key_blk, MOBA_MASK, 0.0).astype(BF16)
        va_ref[:, hd:] = jnp.where(lane == 0, 1.0, 0.0).astype(BF16)

    q = q_ref[0, 0]
    k = k_ref[0, 0]
    qa_ref[:, :hd] = q
    ka_ref[:, :hd] = k
    va_ref[:, :hd] = v_ref[0, 0]

    kbar = jnp.sum(k.astype(F32).reshape(nblk, blk, hd), axis=1) * (1.0 / blk)
    k1 = kbar.astype(BF16)
    r1 = kbar - k1.astype(F32)
    k2 = r1.astype(BF16)
    k3 = (r1 - k2.astype(F32)).astype(BF16)
    gate = _dot_nt(k1, q) + _dot_nt(k2, q) + _dot_nt(k3, q)
    n_idx = lax.broadcasted_iota(jnp.int32, gate.shape, 0)
    own = lax.broadcasted_iota(jnp.int32, gate.shape, 1) // blk
    gate = jnp.where(n_idx < own, gate, -jnp.inf)
    skip = jnp.where(n_idx == own, 0.0, 1.0)
    for _ in range(min(MOBA_TOPK, nblk)):
        best = jnp.max(gate, axis=0, keepdims=True)
        idx = jnp.min(jnp.where(gate == best, n_idx, nblk), axis=0, keepdims=True)
        skip = jnp.where((n_idx == idx) & (best > -jnp.inf), 0.0, skip)
        gate = jnp.where(n_idx == idx, -jnp.inf, gate)
    skip = jnp.concatenate([skip, jnp.zeros((hd - nblk, seq), F32)], axis=0)
    qa_ref[:, hd:] = skip.T.astype(BF16)

    for i in range(nblk):
        width = (i + 1) * blk
        rows = slice(i * blk, (i + 1) * blk)
        s = _dot_nt(qa_ref[rows, :], ka_ref[:width, :])
        s = s + jnp.concatenate([bias_ref[i - n] for n in range(i + 1)], axis=1)
        m = jnp.max(s, axis=-1, keepdims=True)
        p = jnp.exp2(s - m).astype(BF16)
        o = _dot(p, va_ref[:width, :])
        o_ref[0, rows, :] = (o[:, :hd] / o[:, hd:hd + 1]).astype(o_ref.dtype)


def _moba(qkv, rel_bias, *, batch, seq, head0=N_HEADS_A, n_heads=N_HEADS - N_HEADS_A):
    nh_all = qkv.shape[0] // 3
    blk = MOBA_BLOCK
    nblk = seq // blk
    assert nblk <= HEAD_DIM
    view = qkv.reshape(3 * nh_all, batch, seq, HEAD_DIM)
    return pl.pallas_call(
        functools.partial(_moba_kernel, seq=seq, head0=head0),
        grid=(n_heads, batch),
        in_specs=[pl.BlockSpec(memory_space=pltpu.SMEM)] + [
            pl.BlockSpec((1, 1, seq, HEAD_DIM), lambda h, b, j=j: (j * nh_all + head0 + h, b, 0, 0))
            for j in range(3)],
        out_specs=pl.BlockSpec((1, seq, HEAD_DIM), lambda h, b: (b, 0, h)),
        out_shape=jax.ShapeDtypeStruct((batch, seq, n_heads * HEAD_DIM), BF16),
        scratch_shapes=[
            pltpu.VMEM((nblk, blk, blk), F32),
            pltpu.VMEM((seq, 2 * HEAD_DIM), BF16),
            pltpu.VMEM((seq, 2 * HEAD_DIM), BF16),
            pltpu.VMEM((seq, 2 * HEAD_DIM), BF16),
        ],
        compiler_params=_cparams(("arbitrary", "arbitrary")),
        name="moba",
    )(rel_bias, view, view, view)


def _sb_kernel(q_ref, k_ref, v_ref, o_ref, c_ref, acc_ref):
    t = SB_T
    i = pl.program_id(2)
    rows = lax.broadcasted_iota(jnp.int32, (t, t), 0)
    cols = lax.broadcasted_iota(jnp.int32, (t, t), 1)
    causal = cols < rows
    suffix = jnp.where(rows > cols, 1.0, 0.0).astype(BF16)
    sign_bit = jnp.uint32(0x80000000)

    def tile(q, krows, hh, keep):
        z = _dot_nt(q, k_ref[hh, 0, krows, :])
        neg_abs = lax.bitcast_convert_type(lax.bitcast_convert_type(z, jnp.uint32) | sign_bit, F32)
        sp = jnp.maximum(z, 0.0) + jnp.log(1.0 + jnp.exp2(neg_abs)) * LOG2E
        nk = sp if keep is None else jnp.where(keep, sp, 0.0)
        logw = (z - sp) - _dot(nk.astype(BF16), suffix)
        if keep is not None:
            logw = jnp.where(keep, logw, -jnp.inf)
        return nk, logw

    has_prev = i > 0
    own_rows = pl.ds(pl.multiple_of(i * t, t), t)
    prev_rows = pl.ds(pl.multiple_of(jnp.maximum(i - 1, 0) * t, t), t)
    for hh in range(SB_HEADS):
        q = q_ref[hh, 0]
        nk_a, lw_a = tile(q, own_rows, hh, causal)
        nk_b, lw_b = tile(q, prev_rows, hh, None)
        c_a = jnp.sum(nk_a, axis=-1, keepdims=True)
        p_a = jnp.exp2(lw_a).astype(BF16)
        p_b = jnp.exp2(jnp.where(has_prev, lw_b - c_a, -jnp.inf)).astype(BF16)
        acc_ref[hh] = _dot(p_a, v_ref[hh, 0, own_rows, :]) + _dot(p_b, v_ref[hh, 0, prev_rows, :])
        c_ref[hh] = c_a + jnp.where(has_prev, jnp.sum(nk_b, axis=-1, keepdims=True), 0.0)

    def cond(state):
        j, cmin = state
        return (j >= 0) & (cmin < SB_DEAD)

    def body(state):
        j, _ = state
        krows = pl.ds(pl.multiple_of(j * t, t), t)
        for hh in range(SB_HEADS):
            nk, lw = tile(q_ref[hh, 0], krows, hh, None)
            acc_ref[hh] += _dot(jnp.exp2(lw - c_ref[hh]).astype(BF16), v_ref[hh, 0, krows, :])
            c_ref[hh] += jnp.sum(nk, axis=-1, keepdims=True)
        return j - 1, jnp.min(c_ref[...])

    lax.while_loop(cond, body, (i - 2, jnp.min(c_ref[...])))
    o_ref[0] = jnp.concatenate([acc_ref[hh] for hh in range(SB_HEADS)], axis=1).astype(o_ref.dtype)


def _stickbreaking(qkv, *, batch, seq):
    nh = qkv.shape[0] // 3
    view = qkv.reshape(3 * nh, batch, seq, HEAD_DIM)
    g = SB_HEADS
    assert nh % g == 0
    return pl.pallas_call(
        _sb_kernel,
        grid=(batch, nh // g, seq // SB_T),
        in_specs=[
            pl.BlockSpec((g, 1, SB_T, HEAD_DIM), lambda b, h, i: (h, b, i, 0)),
            pl.BlockSpec((g, 1, seq, HEAD_DIM), lambda b, h, i: (nh // g + h, b, 0, 0)),
            pl.BlockSpec((g, 1, seq, HEAD_DIM), lambda b, h, i: (2 * (nh // g) + h, b, 0, 0)),
        ],
        out_specs=pl.BlockSpec((1, SB_T, g * HEAD_DIM), lambda b, h, i: (b, i, h)),
        out_shape=jax.ShapeDtypeStruct((batch, seq, nh * HEAD_DIM), BF16),
        scratch_shapes=[pltpu.VMEM((g, SB_T, 1), F32), pltpu.VMEM((g, SB_T, HEAD_DIM), F32)],
        compiler_params=_cparams(("parallel", "parallel", "arbitrary")),
        name="stickbreaking",
    )(view, view, view)


def kernel(x, ln_gains, ffn_w_gate, ffn_w_up, ffn_w_down, w_qkv_even, w_out_even, w_qkv_odd, w_out_odd,
           rel_bias, final_gain):
    batch, seq, d = x.shape
    depth = ln_gains.shape[0]
    m = batch * seq
    wg, wu, wd = (w.astype(BF16) for w in (ffn_w_gate, ffn_w_up, ffn_w_down))
    wqkv = (w_qkv_even.astype(BF16), w_qkv_odd.astype(BF16))
    wout = (w_out_even.astype(BF16), w_out_odd.astype(BF16))
    rel_bias = rel_bias.astype(F32)

    h = x.reshape(m, d)
    for i in range(depth):
        h = _ffn(h, ln_gains[i, 0], wg, wu, wd, i, 0)
        qkv = _qkv(h, ln_gains[i, 1], wqkv[i % 2][i // 2])
        if i % 2 == 0:
            parts = [_dilated(qkv, rel_bias, batch=batch, seq=seq), _moba(qkv, rel_bias, batch=batch, seq=seq)]
        else:
            parts = [_stickbreaking(qkv, batch=batch, seq=seq)]
        parts = [p.reshape(m, p.shape[-1]) for p in parts]
        h = _oproj(h, parts, wout[i % 2][i // 2])
        last = i == depth - 1
        h = _ffn(h, ln_gains[i, 2], wg, wu, wd, i, 1, final_gain if last else None)
    return h.reshape(batch, seq, d)
```

```python
import functools
import math

import jax
import jax.numpy as jnp
from jax import lax
from jax.experimental import pallas as pl
from jax.experimental.pallas import tpu as pltpu

F32 = jnp.float32
BF16 = jnp.bfloat16

HEAD_DIM = 128
N_HEADS = 16
N_HEADS_A = 8
DILATIONS = (1, 4, 16)
DIL_W = 128
DIL_MID = 4
MOBA_BLOCK = 256
MOBA_TOPK = 3
LOG2E = math.log2(math.e)
MOBA_MASK = -(2.0 ** 100)
REL_BUCKETS = 32
REL_MAX_DIST = 2048
FFN_HALF = 0.5
RMS_EPS = 1e-6
NEG = -1e30
SB_T = 256
SB_HEADS = 8
SB_DEAD = 152.0
V7X_VMEM_LIMIT = 56 * 1024 * 1024


def _cparams(sem):
    return pltpu.CompilerParams(dimension_semantics=sem, vmem_limit_bytes=V7X_VMEM_LIMIT)


def _rms(x, g):
    ms = jnp.mean(x * x, axis=-1, keepdims=True)
    return x * lax.rsqrt(ms + RMS_EPS) * g


def _dot(a, b):
    return jnp.dot(a, b, preferred_element_type=F32)


def _serpentine(outer, inner, n_inner):
    return jnp.where(outer % 2 == 0, inner, n_inner - 1 - inner)


def _dot_nt(a, b):
    return lax.dot_general(a, b, (((1,), (1,)), ((), ())), preferred_element_type=F32)


def _ffn_kernel(h_ref, g_ref, wg_ref, wu_ref, wd_ref, *rest, nf, tail, tail_step, dn, final):
    if final:
        fg_ref, o_ref, n_ref = rest
    else:
        o_ref, n_ref = rest
    f = pl.program_id(1)
    tf = wg_ref.shape[1]
    d = o_ref.shape[1]

    def chunk(width, first, last):
        if first:
            n = _rms(h_ref[...], g_ref[...]).astype(BF16)
            n_ref[...] = n
        else:
            n = n_ref[...]
        gate = _dot(n, wg_ref[:, :width])
        up = _dot(n, wu_ref[:, :width])
        a = (gate * (1.0 / (1.0 + jnp.exp(-gate))) * up).astype(BF16)
        for c in range(d // dn):
            sl = slice(c * dn, (c + 1) * dn)
            acc = _dot(a, wd_ref[:width, sl])
            if not first:
                acc = o_ref[:, sl] + acc
            o_ref[:, sl] = h_ref[:, sl] + FFN_HALF * acc if last else acc
        if last and final:
            o_ref[...] = _rms(o_ref[...], fg_ref[...])

    variants = {}
    for step in range(nf):
        key = (tail if step == tail_step else tf, step == 0, step == nf - 1)
        variants.setdefault(key, []).append(step)
    for (width, first, last), steps in variants.items():
        hit = functools.reduce(jnp.logical_or, [f == s for s in steps])
        pl.when(hit)(functools.partial(chunk, width, first, last))


def _ffn(h, gain, wg, wu, wd, layer, half, final_gain=None, *, tm=512, tf=1024, dn=512):
    m, d = h.shape
    dff = wg.shape[-1]
    nf = pl.cdiv(dff, tf)
    tail = dff - (nf - 1) * tf
    tail_step = (nf - 1) // 2
    final = final_gain is not None

    def chunk(f):
        return jnp.where(f == tail_step, nf - 1, jnp.where(f < tail_step, f, f - 1))

    in_specs = [
        pl.BlockSpec((tm, d), lambda i, f: (i, 0)),
        pl.BlockSpec((1, d), lambda i, f: (0, 0)),
        pl.BlockSpec((None, None, d, tf), lambda i, f: (layer, half, 0, chunk(f))),
        pl.BlockSpec((None, None, d, tf), lambda i, f: (layer, half, 0, chunk(f))),
        pl.BlockSpec((None, None, tf, d), lambda i, f: (layer, half, chunk(f), 0)),
    ]
    args = [h, gain.reshape(1, d), wg, wu, wd]
    if final:
        in_specs.append(pl.BlockSpec((1, d), lambda i, f: (0, 0)))
        args.append(final_gain.reshape(1, d))
    return pl.pallas_call(
        functools.partial(_ffn_kernel, nf=nf, tail=tail, tail_step=tail_step, dn=dn, final=final),
        grid=(m // tm, nf),
        in_specs=in_specs,
        out_specs=pl.BlockSpec((tm, d), lambda i, f: (i, 0)),
        out_shape=jax.ShapeDtypeStruct((m, d), F32),
        scratch_shapes=[pltpu.VMEM((tm, d), BF16)],
        compiler_params=_cparams(("parallel", "arbitrary")),
        name="ffn_final" if final else "ffn",
    )(*args)


def _qkv_kernel(h_ref, g_ref, w_ref, o_ref, n_ref, *, q_blocks, n_blocks, scale):
    col = _serpentine(pl.program_id(0), pl.program_id(1), n_blocks)

    def project(first):
        if first:
            n = _rms(h_ref[...], g_ref[...]).astype(BF16)
            n_ref[...] = n
        else:
            n = n_ref[...]
        acc = _dot(n, w_ref[...])
        acc = acc * jnp.where(col < q_blocks, scale, 1.0).astype(F32)
        for c in range(o_ref.shape[0]):
            o_ref[c] = acc[:, c * HEAD_DIM:(c + 1) * HEAD_DIM].astype(BF16)

    pl.when(pl.program_id(1) == 0)(functools.partial(project, True))
    pl.when(pl.program_id(1) > 0)(functools.partial(project, False))


def _qkv(h, gain, w, *, tm=1024, tn=1024):
    m, d = h.shape
    n3 = w.shape[1]
    hpb = tn // HEAD_DIM
    nj = n3 // tn
    return pl.pallas_call(
        functools.partial(_qkv_kernel, q_blocks=(n3 // 3) // tn, n_blocks=nj, scale=HEAD_DIM ** -0.5 * LOG2E),
        grid=(m // tm, nj),
        in_specs=[
            pl.BlockSpec((tm, d), lambda i, j: (i, 0)),
            pl.BlockSpec((1, d), lambda i, j: (0, 0)),
            pl.BlockSpec((d, tn), lambda i, j: (0, _serpentine(i, j, nj))),
        ],
        out_specs=pl.BlockSpec((hpb, tm, HEAD_DIM), lambda i, j: (_serpentine(i, j, nj), i, 0)),
        out_shape=jax.ShapeDtypeStruct((n3 // HEAD_DIM, m, HEAD_DIM), BF16),
        scratch_shapes=[pltpu.VMEM((tm, d), BF16)],
        compiler_params=_cparams(("parallel", "arbitrary")),
        name="qkv",
    )(h, gain.reshape(1, d), w)


def _oproj_kernel(h_ref, *rest, n_in):
    a_refs = rest[:n_in]
    w_ref = rest[n_in]
    o_ref = rest[n_in + 1]
    acc = h_ref[...]
    off = 0
    for a_ref in a_refs:
        k = a_ref.shape[1]
        acc = acc + _dot(a_ref[...], w_ref[off:off + k, :])
        off += k
    o_ref[...] = acc


def _oproj(h, attn_parts, w, *, tm=512):
    m, d = h.shape
    in_specs = [pl.BlockSpec((tm, d), lambda i: (i, 0))]
    for a in attn_parts:
        in_specs.append(pl.BlockSpec((tm, a.shape[1]), lambda i: (i, 0)))
    in_specs.append(pl.BlockSpec(w.shape, lambda i: (0, 0)))
    return pl.pallas_call(
        functools.partial(_oproj_kernel, n_in=len(attn_parts)),
        grid=(m // tm,),
        in_specs=in_specs,
        out_specs=pl.BlockSpec((tm, d), lambda i: (i, 0)),
        out_shape=jax.ShapeDtypeStruct((m, d), F32),
        compiler_params=_cparams(("parallel",)),
        name="oproj",
    )(h, *attn_parts, w)


def _bucket_of(dist):
    max_exact = REL_BUCKETS // 2
    d = max(dist, 0)
    if d < max_exact:
        return d
    large = max_exact + int(math.log(d / max_exact) / math.log(REL_MAX_DIST / max_exact) * (REL_BUCKETS - max_exact))
    return min(large, REL_BUCKETS - 1)


def _bucket_bias(dist, tab_ref, head, dist_min, dist_max):
    max_exact = REL_BUCKETS // 2
    d = jnp.maximum(dist, 0)
    df = jnp.maximum(d, 1).astype(F32)
    large = max_exact + (jnp.log(df / max_exact) / math.log(REL_MAX_DIST / max_exact)
                         * (REL_BUCKETS - max_exact)).astype(jnp.int32)
    large = jnp.minimum(large, REL_BUCKETS - 1)
    bucket = jnp.where(d < max_exact, d, large)
    lo = max(_bucket_of(dist_min) - 1, 0)
    hi = min(_bucket_of(dist_max) + 1, REL_BUCKETS - 1)
    out = jnp.full(dist.shape, tab_ref[lo, head], F32)
    for b in range(lo + 1, hi + 1):
        out = jnp.where(bucket == b, tab_ref[b, head], out)
    return out


def _dilated_kernel(tab_ref, q_ref, k_ref, v_ref, o_ref, bias_ref, stage_ref, mid_ref, num_ref, lse_ref,
                    *, seq, head0):
    nbr = len(DILATIONS)
    h = pl.program_id(0)
    b = pl.program_id(1)
    w = DIL_W
    j = lax.broadcasted_iota(jnp.int32, (w, 2 * w), 1)

    @pl.when(b == 0)
    def _():
        i = lax.broadcasted_iota(jnp.int32, (w, 2 * w), 0)
        rel = w + i - j
        ok = (rel >= 0) & (rel <= w)
        for br, r in enumerate(DILATIONS):
            bias_ref[br] = jnp.where(ok, _bucket_bias(rel * r, tab_ref, head0 + h, 0, w * r) * LOG2E, NEG)

    no_prev = jnp.where(j < w, NEG, 0.0).astype(F32)

    mid = DIL_MID
    for t, ref in enumerate((q_ref, k_ref, v_ref)):
        stage_ref[t] = ref[0, 0].astype(F32)
        for c in range(mid):
            mid_ref[t, c] = stage_ref[t, pl.ds(c, seq // mid, stride=mid), :]

    def subsequence(t, r, c):
        if r == mid:
            return mid_ref[t, c].astype(BF16)
        return mid_ref[t, c % mid, pl.ds(c // mid, seq // r, stride=r // mid), :].astype(BF16)

    for br, r in enumerate(DILATIONS):
        sub = seq // r
        nb = sub // w
        for c in range(r):
            rows = pl.ds(c, sub, stride=r) if r > 1 else slice(None)
            if r > 1:
                qc, kc, vc = (subsequence(t, r, c) for t in range(3))
            else:
                qc, kc, vc = q_ref[0, 0], k_ref[0, 0], v_ref[0, 0]

            def band(t):
                prev = jnp.concatenate([t[:w], t[:sub - w]], axis=0) if nb > 1 else t
                return jnp.concatenate([prev.reshape(nb, w, HEAD_DIM), t.reshape(nb, w, HEAD_DIM)], axis=1)

            s = lax.dot_general(qc.reshape(nb, w, HEAD_DIM), band(kc), (((2,), (2,)), ((0,), (0,))),
                                preferred_element_type=F32) + bias_ref[br][None]
            s = jnp.concatenate([s[:1] + no_prev[None], s[1:]], axis=0) if nb > 1 else s + no_prev[None]
            m = jnp.max(s, axis=-1, keepdims=True)
            p = jnp.exp2(s - m)
            den = jnp.sum(p, axis=-1, keepdims=True)
            o = lax.dot_general(p.astype(BF16), band(vc), (((2,), (1,)), ((0,), (0,))),
                                preferred_element_type=F32) / den
            lse = m + jnp.log(den) * LOG2E
            num_ref[br, rows, :] = o.reshape(sub, HEAD_DIM)
            lse_ref[br, rows, :] = jnp.broadcast_to(lse, (nb, w, HEAD_DIM)).reshape(sub, HEAD_DIM)

    ch = 512

    def combine(t, carry):
        rows = pl.ds(pl.multiple_of(t * ch, ch), ch)
        ls = [lse_ref[br, rows, :] for br in range(nbr)]
        m = functools.reduce(jnp.maximum, ls)
        ws = [jnp.exp2(l - m) for l in ls]
        tot = functools.reduce(lambda x, y: x + y, ws)
        acc = functools.reduce(lambda x, y: x + y, [wt * num_ref[br, rows, :] for br, wt in enumerate(ws)])
        o_ref[0, rows, :] = (acc / tot).astype(o_ref.dtype)
        return carry

    lax.fori_loop(0, seq // ch, combine, 0)


def _dilated(qkv, rel_bias, *, batch, seq, head0=0, n_heads=N_HEADS_A):
    nh_all = qkv.shape[0] // 3
    view = qkv.reshape(3 * nh_all, batch, seq, HEAD_DIM)
    nbr = len(DILATIONS)
    return pl.pallas_call(
        functools.partial(_dilated_kernel, seq=seq, head0=head0),
        grid=(n_heads, batch),
        in_specs=[pl.BlockSpec(memory_space=pltpu.SMEM)] + [
            pl.BlockSpec((1, 1, seq, HEAD_DIM), lambda h, b, j=j: (j * nh_all + head0 + h, b, 0, 0))
            for j in range(3)],
        out_specs=pl.BlockSpec((1, seq, HEAD_DIM), lambda h, b: (b, 0, h)),
        out_shape=jax.ShapeDtypeStruct((batch, seq, n_heads * HEAD_DIM), BF16),
        scratch_shapes=[
            pltpu.VMEM((nbr, DIL_W, 2 * DIL_W), F32),
            pltpu.VMEM((3, seq, HEAD_DIM), F32),
            pltpu.VMEM((3, DIL_MID, seq // DIL_MID, HEAD_DIM), F32),
            pltpu.VMEM((nbr, seq, HEAD_DIM), F32),
            pltpu.VMEM((nbr, seq, HEAD_DIM), F32),
        ],
        compiler_params=_cparams(("arbitrary", "arbitrary")),
        name="dilated",
    )(rel_bias, view, view, view)


def _moba_kernel(tab_ref, q_ref, k_ref, v_ref, o_ref, bias_ref, qa_ref, ka_ref, va_ref, *, seq, head0):
    blk = MOBA_BLOCK
    nblk = seq // blk
    hd = HEAD_DIM
    h = pl.program_id(0)
    b = pl.program_id(1)

    @pl.when(b == 0)
    def _():
        r = lax.broadcasted_iota(jnp.int32, (blk, blk), 0)
        c = lax.broadcasted_iota(jnp.int32, (blk, blk), 1)

        for dlt in range(nblk):
            dist = dlt * blk + r - c
            bias = _bucket_bias(dist, tab_ref, head0 + h, max(dlt * blk - (blk - 1), 0), dlt * blk + blk - 1)
            bias_ref[dlt] = jnp.where(dist >= 0, bias * LOG2E, NEG)

    @pl.when((h == 0) & (b == 0))
    def _():
        key_blk = lax.broadcasted_iota(jnp.int32, (seq, hd), 0) // blk
        lane = lax.broadcasted_iota(jnp.int32, (seq, hd), 1)
        ka_ref[:, hd:] = jnp.where(lane == key_blk, MOBA_MASK, 0.0).astype(BF16)
        va_ref[:, hd:] = jnp.where(lane == 0, 1.0, 0.0).astype(BF16)

    q = q_ref[0, 0]
    k = k_ref[0, 0]
    qa_ref[:, :hd] = q
    ka_ref[:, :hd] = k
    va_ref[:, :hd] = v_ref[0, 0]

    kbar = jnp.sum(k.astype(F32).reshape(nblk, blk, hd), axis=1) * (1.0 / blk)
    k1 = kbar.astype(BF16)
    r1 = kbar - k1.astype(F32)
    k2 = r1.astype(BF16)
    k3 = (r1 - k2.astype(F32)).astype(BF16)
    gate = _dot_nt(k1, q) + _dot_nt(k2, q) + _dot_nt(k3, q)
    n_idx = lax.broadcasted_iota(jnp.int32, gate.shape, 0)
    own = lax.broadcasted_iota(jnp.int32, gate.shape, 1) // blk
    gate = jnp.where(n_idx < own, gate, -jnp.inf)
    skip = jnp.where(n_idx == own, 0.0, 1.0)
    for _ in range(min(MOBA_TOPK, nblk)):
        best = jnp.max(gate, axis=0, keepdims=True)
        idx = jnp.min(jnp.where(gate == best, n_idx, nblk), axis=0, keepdims=True)
        skip = jnp.where((n_idx == idx) & (best > -jnp.inf), 0.0, skip)
        gate = jnp.where(n_idx == idx, -jnp.inf, gate)
    skip = jnp.concatenate([skip, jnp.zeros((hd - nblk, seq), F32)], axis=0)
    qa_ref[:, hd:] = skip.T.astype(BF16)

    for i in range(nblk):
        width = (i + 1) * blk
        rows = slice(i * blk, (i + 1) * blk)
        s = _dot_nt(qa_ref[rows, :], ka_ref[:width, :])
        s = s + jnp.concatenate([bias_ref[i - n] for n in range(i + 1)], axis=1)
        m = jnp.max(s, axis=-1, keepdims=True)
        p = jnp.exp2(s - m).astype(BF16)
        o = _dot(p, va_ref[:width, :])
        o_ref[0, rows, :] = (o[:, :hd] / o[:, hd:hd + 1]).astype(o_ref.dtype)


def _moba(qkv, rel_bias, *, batch, seq, head0=N_HEADS_A, n_heads=N_HEADS - N_HEADS_A):
    nh_all = qkv.shape[0] // 3
    blk = MOBA_BLOCK
    nblk = seq // blk
    assert nblk <= HEAD_DIM
    view = qkv.reshape(3 * nh_all, batch, seq, HEAD_DIM)
    return pl.pallas_call(
        functools.partial(_moba_kernel, seq=seq, head0=head0),
        grid=(n_heads, batch),
        in_specs=[pl.BlockSpec(memory_space=pltpu.SMEM)] + [
            pl.BlockSpec((1, 1, seq, HEAD_DIM), lambda h, b, j=j: (j * nh_all + head0 + h, b, 0, 0))
            for j in range(3)],
        out_specs=pl.BlockSpec((1, seq, HEAD_DIM), lambda h, b: (b, 0, h)),
        out_shape=jax.ShapeDtypeStruct((batch, seq, n_heads * HEAD_DIM), BF16),
        scratch_shapes=[
            pltpu.VMEM((nblk, blk, blk), F32),
            pltpu.VMEM((seq, 2 * HEAD_DIM), BF16),
            pltpu.VMEM((seq, 2 * HEAD_DIM), BF16),
            pltpu.VMEM((seq, 2 * HEAD_DIM), BF16),
        ],
        compiler_params=_cparams(("arbitrary", "arbitrary")),
        name="moba",
    )(rel_bias, view, view, view)


def _sb_kernel(q_ref, k_ref, v_ref, o_ref, c_ref, acc_ref):
    t = SB_T
    i = pl.program_id(2)
    rows = lax.broadcasted_iota(jnp.int32, (t, t), 0)
    cols = lax.broadcasted_iota(jnp.int32, (t, t), 1)
    causal = cols < rows
    suffix = jnp.where(rows > cols, 1.0, 0.0).astype(BF16)
    sign_bit = jnp.uint32(0x80000000)

    def tile(q, krows, hh, keep):
        z = _dot_nt(q, k_ref[hh, 0, krows, :])
        neg_abs = lax.bitcast_convert_type(lax.bitcast_convert_type(z, jnp.uint32) | sign_bit, F32)
        sp = jnp.maximum(z, 0.0) + jnp.log(1.0 + jnp.exp2(neg_abs)) * LOG2E
        nk = sp if keep is None else jnp.where(keep, sp, 0.0)
        logw = (z - sp) - _dot(nk.astype(BF16), suffix)
        if keep is not None:
            logw = jnp.where(keep, logw, -jnp.inf)
        return nk, logw

    has_prev = i > 0
    own_rows = pl.ds(pl.multiple_of(i * t, t), t)
    prev_rows = pl.ds(pl.multiple_of(jnp.maximum(i - 1, 0) * t, t), t)
    for hh in range(SB_HEADS):
        q = q_ref[hh, 0]
        nk_a, lw_a = tile(q, own_rows, hh, causal)
        nk_b, lw_b = tile(q, prev_rows, hh, None)
        c_a = jnp.sum(nk_a, axis=-1, keepdims=True)
        p_a = jnp.exp2(lw_a).astype(BF16)
        p_b = jnp.exp2(jnp.where(has_prev, lw_b - c_a, -jnp.inf)).astype(BF16)
        acc_ref[hh] = _dot(p_a, v_ref[hh, 0, own_rows, :]) + _dot(p_b, v_ref[hh, 0, prev_rows, :])
        c_ref[hh] = c_a + jnp.where(has_prev, jnp.sum(nk_b, axis=-1, keepdims=True), 0.0)

    def cond(state):
        j, cmin = state
        return (j >= 0) & (cmin < SB_DEAD)

    def body(state):
        j, _ = state
        krows = pl.ds(pl.multiple_of(j * t, t), t)
        for hh in range(SB_HEADS):
            nk, lw = tile(q_ref[hh, 0], krows, hh, None)
            acc_ref[hh] += _dot(jnp.exp2(lw - c_ref[hh]).astype(BF16), v_ref[hh, 0, krows, :])
            c_ref[hh] += jnp.sum(nk, axis=-1, keepdims=True)
        return j - 1, jnp.min(c_ref[...])

    lax.while_loop(cond, body, (i - 2, jnp.min(c_ref[...])))
    o_ref[0] = jnp.concatenate([acc_ref[hh] for hh in range(SB_HEADS)], axis=1).astype(o_ref.dtype)


def _stickbreaking(qkv, *, batch, seq):
    nh = qkv.shape[0] // 3
    view = qkv.reshape(3 * nh, batch, seq, HEAD_DIM)
    g = SB_HEADS
    assert nh % g == 0
    return pl.pallas_call(
        _sb_kernel,
        grid=(batch, nh // g, seq // SB_T),
        in_specs=[
            pl.BlockSpec((g, 1, SB_T, HEAD_DIM), lambda b, h, i: (h, b, i, 0)),
            pl.BlockSpec((g, 1, seq, HEAD_DIM), lambda b, h, i: (nh // g + h, b, 0, 0)),
            pl.BlockSpec((g, 1, seq, HEAD_DIM), lambda b, h, i: (2 * (nh // g) + h, b, 0, 0)),
        ],
        out_specs=pl.BlockSpec((1, SB_T, g * HEAD_DIM), lambda b, h, i: (b, i, h)),
        out_shape=jax.ShapeDtypeStruct((batch, seq, nh * HEAD_DIM), BF16),
        scratch_shapes=[pltpu.VMEM((g, SB_T, 1), F32), pltpu.VMEM((g, SB_T, HEAD_DIM), F32)],
        compiler_params=_cparams(("parallel", "parallel", "arbitrary")),
        name="stickbreaking",
    )(view, view, view)


def kernel(x, ln_gains, ffn_w_gate, ffn_w_up, ffn_w_down, w_qkv_even, w_out_even, w_qkv_odd, w_out_odd,
           rel_bias, final_gain):
    batch, seq, d = x.shape
    depth = ln_gains.shape[0]
    m = batch * seq
    wg, wu, wd = (w.astype(BF16) for w in (ffn_w_gate, ffn_w_up, ffn_w_down))
    wqkv = (w_qkv_even.astype(BF16), w_qkv_odd.astype(BF16))
    wout = (w_out_even.astype(BF16), w_out_odd.astype(BF16))
    rel_bias = rel_bias.astype(F32)

    h = x.reshape(m, d)
    for i in range(depth):
        h = _ffn(h, ln_gains[i, 0], wg, wu, wd, i, 0)
        qkv = _qkv(h, ln_gains[i, 1], wqkv[i % 2][i // 2])
        if i % 2 == 0:
            parts = [_dilated(qkv, rel_bias, batch=batch, seq=seq), _moba(qkv, rel_bias, batch=batch, seq=seq)]
        else:
            parts = [_stickbreaking(qkv, batch=batch, seq=seq)]
        parts = [p.reshape(m, p.shape[-1]) for p in parts]
        h = _oproj(h, parts, wout[i % 2][i // 2])
        last = i == depth - 1
        h = _ffn(h, ln_gains[i, 2], wg, wu, wd, i, 1, final_gain if last else None)
    return h.reshape(batch, seq, d)
```

```python
import functools
import math

import jax
import jax.numpy as jnp
from jax import lax
from jax.experimental import pallas as pl
from jax.experimental.pallas import tpu as pltpu

F32 = jnp.float32
BF16 = jnp.bfloat16

HEAD_DIM = 128
N_HEADS = 16
N_HEADS_A = 8
DILATIONS = (1, 4, 16)
DIL_W = 128
DIL_MID = 4
MOBA_BLOCK = 256
MOBA_TOPK = 3
LOG2E = math.log2(math.e)
MOBA_MASK = -(2.0 ** 100)
REL_BUCKETS = 32
REL_MAX_DIST = 2048
FFN_HALF = 0.5
RMS_EPS = 1e-6
NEG = -1e30
SB_T = 256
SB_HEADS = 8
SB_DEAD = 152.0
V7X_VMEM_LIMIT = 56 * 1024 * 1024


def _cparams(sem):
    return pltpu.CompilerParams(dimension_semantics=sem, vmem_limit_bytes=V7X_VMEM_LIMIT)


def _rms(x, g):
    ms = jnp.mean(x * x, axis=-1, keepdims=True)
    return x * lax.rsqrt(ms + RMS_EPS) * g


def _dot(a, b):
    return jnp.dot(a, b, preferred_element_type=F32)


def _serpentine(outer, inner, n_inner):
    return jnp.where(outer % 2 == 0, inner, n_inner - 1 - inner)


def _dot_nt(a, b):
    return lax.dot_general(a, b, (((1,), (1,)), ((), ())), preferred_element_type=F32)


def _ffn_kernel(h_ref, g_ref, wg_ref, wu_ref, wd_ref, *rest, nf, tail, tail_step, dn, final):
    if final:
        fg_ref, o_ref, n_ref = rest
    else:
        o_ref, n_ref = rest
    f = pl.program_id(1)
    tf = wg_ref.shape[1]
    d = o_ref.shape[1]

    def chunk(width, first, last):
        if first:
            n = _rms(h_ref[...], g_ref[...]).astype(BF16)
            n_ref[...] = n
        else:
            n = n_ref[...]
        gate = _dot(n, wg_ref[:, :width])
        up = _dot(n, wu_ref[:, :width])
        a = (gate * (1.0 / (1.0 + jnp.exp(-gate))) * up).astype(BF16)
        for c in range(d // dn):
            sl = slice(c * dn, (c + 1) * dn)
            acc = _dot(a, wd_ref[:width, sl])
            if not first:
                acc = o_ref[:, sl] + acc
            o_ref[:, sl] = h_ref[:, sl] + FFN_HALF * acc if last else acc
        if last and final:
            o_ref[...] = _rms(o_ref[...], fg_ref[...])

    variants = {}
    for step in range(nf):
        key = (tail if step == tail_step else tf, step == 0, step == nf - 1)
        variants.setdefault(key, []).append(step)
    for (width, first, last), steps in variants.items():
        hit = functools.reduce(jnp.logical_or, [f == s for s in steps])
        pl.when(hit)(functools.partial(chunk, width, first, last))


def _ffn(h, gain, wg, wu, wd, layer, half, final_gain=None, *, tm=512, tf=1024, dn=512):
    m, d = h.shape
    dff = wg.shape[-1]
    nf = pl.cdiv(dff, tf)
    tail = dff - (nf - 1) * tf
    tail_step = (nf - 1) // 2
    final = final_gain is not None

    def chunk(f):
        return jnp.where(f == tail_step, nf - 1, jnp.where(f < tail_step, f, f - 1))

    in_specs = [
        pl.BlockSpec((tm, d), lambda i, f: (i, 0)),
        pl.BlockSpec((1, d), lambda i, f: (0, 0)),
        pl.BlockSpec((None, None, d, tf), lambda i, f: (layer, half, 0, chunk(f))),
        pl.BlockSpec((None, None, d, tf), lambda i, f: (layer, half, 0, chunk(f))),
        pl.BlockSpec((None, None, tf, d), lambda i, f: (layer, half, chunk(f), 0)),
    ]
    args = [h, gain.reshape(1, d), wg, wu, wd]
    if final:
        in_specs.append(pl.BlockSpec((1, d), lambda i, f: (0, 0)))
        args.append(final_gain.reshape(1, d))
    return pl.pallas_call(
        functools.partial(_ffn_kernel, nf=nf, tail=tail, tail_step=tail_step, dn=dn, final=final),
        grid=(m // tm, nf),
        in_specs=in_specs,
        out_specs=pl.BlockSpec((tm, d), lambda i, f: (i, 0)),
        out_shape=jax.ShapeDtypeStruct((m, d), F32),
        scratch_shapes=[pltpu.VMEM((tm, d), BF16)],
        compiler_params=_cparams(("parallel", "arbitrary")),
        name="ffn_final" if final else "ffn",
    )(*args)


def _qkv_kernel(h_ref, g_ref, w_ref, o_ref, n_ref, *, q_blocks, n_blocks, scale):
    col = _serpentine(pl.program_id(0), pl.program_id(1), n_blocks)

    def project(first):
        if first:
            n = _rms(h_ref[...], g_ref[...]).astype(BF16)
            n_ref[...] = n
        else:
            n = n_ref[...]
        acc = _dot(n, w_ref[...])
        acc = acc * jnp.where(col < q_blocks, scale, 1.0).astype(F32)
        for c in range(o_ref.shape[0]):
            o_ref[c] = acc[:, c * HEAD_DIM:(c + 1) * HEAD_DIM].astype(BF16)

    pl.when(pl.program_id(1) == 0)(functools.partial(project, True))
    pl.when(pl.program_id(1) > 0)(functools.partial(project, False))


def _qkv(h, gain, w, *, tm=1024, tn=1024):
    m, d = h.shape
    n3 = w.shape[1]
    hpb = tn // HEAD_DIM
    nj = n3 // tn
    return pl.pallas_call(
        functools.partial(_qkv_kernel, q_blocks=(n3 // 3) // tn, n_blocks=nj, scale=HEAD_DIM ** -0.5 * LOG2E),
        grid=(m // tm, nj),
        in_specs=[
            pl.BlockSpec((tm, d), lambda i, j: (i, 0)),
            pl.BlockSpec((1, d), lambda i, j: (0, 0)),
            pl.BlockSpec((d, tn), lambda i, j: (0, _serpentine(i, j, nj))),
        ],
        out_specs=pl.BlockSpec((hpb, tm, HEAD_DIM), lambda i, j: (_serpentine(i, j, nj), i, 0)),
        out_shape=jax.ShapeDtypeStruct((n3 // HEAD_DIM, m, HEAD_DIM), BF16),
        scratch_shapes=[pltpu.VMEM((tm, d), BF16)],
        compiler_params=_cparams(("parallel", "arbitrary")),
        name="qkv",
    )(h, gain.reshape(1, d), w)


def _oproj_kernel(h_ref, *rest, n_in):
    a_refs = rest[:n_in]
    w_ref = rest[n_in]
    o_ref = rest[n_in + 1]
    acc = h_ref[...]
    off = 0
    for a_ref in a_refs:
        k = a_ref.shape[1]
        acc = acc + _dot(a_ref[...], w_ref[off:off + k, :])
        off += k
    o_ref[...] = acc


def _oproj(h, attn_parts, w, *, tm=512):
    m, d = h.shape
    in_specs = [pl.BlockSpec((tm, d), lambda i: (i, 0))]
    for a in attn_parts:
        in_specs.append(pl.BlockSpec((tm, a.shape[1]), lambda i: (i, 0)))
    in_specs.append(pl.BlockSpec(w.shape, lambda i: (0, 0)))
    return pl.pallas_call(
        functools.partial(_oproj_kernel, n_in=len(attn_parts)),
        grid=(m // tm,),
        in_specs=in_specs,
        out_specs=pl.BlockSpec((tm, d), lambda i: (i, 0)),
        out_shape=jax.ShapeDtypeStruct((m, d), F32),
        compiler_params=_cparams(("parallel",)),
        name="oproj",
    )(h, *attn_parts, w)


def _bucket_of(dist):
    max_exact = REL_BUCKETS // 2
    d = max(dist, 0)
    if d < max_exact:
        return d
    large = max_exact + int(math.log(d / max_exact) / math.log(REL_MAX_DIST / max_exact) * (REL_BUCKETS - max_exact))
    return min(large, REL_BUCKETS - 1)


def _bucket_bias(dist, tab_ref, head, dist_min, dist_max):
    max_exact = REL_BUCKETS // 2
    d = jnp.maximum(dist, 0)
    df = jnp.maximum(d, 1).astype(F32)
    large = max_exact + (jnp.log(df / max_exact) / math.log(REL_MAX_DIST / max_exact)
                         * (REL_BUCKETS - max_exact)).astype(jnp.int32)
    large = jnp.minimum(large, REL_BUCKETS - 1)
    bucket = jnp.where(d < max_exact, d, large)
    lo = max(_bucket_of(dist_min) - 1, 0)
    hi = min(_bucket_of(dist_max) + 1, REL_BUCKETS - 1)
    out = jnp.full(dist.shape, tab_ref[lo, head], F32)
    for b in range(lo + 1, hi + 1):
        out = jnp.where(bucket == b, tab_ref[b, head], out)
    return out


def _dilated_kernel(tab_ref, q_ref, k_ref, v_ref, o_ref, bias_ref, stage_ref, mid_ref, num_ref, lse_ref,
                    *, seq, head0):
    nbr = len(DILATIONS)
    h = pl.program_id(0)
    b = pl.program_id(1)
    w = DIL_W
    j = lax.broadcasted_iota(jnp.int32, (w, 2 * w), 1)

    @pl.when(b == 0)
    def _():
        i = lax.broadcasted_iota(jnp.int32, (w, 2 * w), 0)
        rel = w + i - j
        ok = (rel >= 0) & (rel <= w)
        for br, r in enumerate(DILATIONS):
            bias_ref[br] = jnp.where(ok, _bucket_bias(rel * r, tab_ref, head0 + h, 0, w * r) * LOG2E, NEG)

    no_prev = jnp.where(j < w, NEG, 0.0).astype(F32)

    mid = DIL_MID
    quarter = seq // mid
    assert DILATIONS[0] == 1 and all(r % mid == 0 for r in DILATIONS[1:])
    for t, ref in enumerate((q_ref, k_ref, v_ref)):
        stage_ref[t] = ref[0, 0].astype(F32)
        for c in range(mid):
            mid_ref[t, c] = stage_ref[t, pl.ds(c, seq // mid, stride=mid), :]

    def subsequence(t, r, c):
        if r == mid:
            return mid_ref[t, c].astype(BF16)
        return mid_ref[t, c % mid, pl.ds(c // mid, seq // r, stride=r // mid), :].astype(BF16)

    for br, r in enumerate(DILATIONS):
        sub = seq // r
        nb = sub // w
        for c in range(r):
            if r == 1:
                rows = slice(None)
            elif r == mid:
                rows = slice(c * quarter, (c + 1) * quarter)
            else:
                rows = pl.ds((c % mid) * quarter + c // mid, sub, stride=r // mid)
            if r > 1:
                qc, kc, vc = (subsequence(t, r, c) for t in range(3))
            else:
                qc, kc, vc = q_ref[0, 0], k_ref[0, 0], v_ref[0, 0]

            def band(t):
                prev = jnp.concatenate([t[:w], t[:sub - w]], axis=0) if nb > 1 else t
                return jnp.concatenate([prev.reshape(nb, w, HEAD_DIM), t.reshape(nb, w, HEAD_DIM)], axis=1)

            s = lax.dot_general(qc.reshape(nb, w, HEAD_DIM), band(kc), (((2,), (2,)), ((0,), (0,))),
                                preferred_element_type=F32) + bias_ref[br][None]
            s = jnp.concatenate([s[:1] + no_prev[None], s[1:]], axis=0) if nb > 1 else s + no_prev[None]
            m = jnp.max(s, axis=-1, keepdims=True)
            p = jnp.exp2(s - m)
            den = jnp.sum(p, axis=-1, keepdims=True)
            o = lax.dot_general(p.astype(BF16), band(vc), (((2,), (1,)), ((0,), (0,))),
                                preferred_element_type=F32) / den
            lse = m + jnp.log(den) * LOG2E
            num_ref[br, rows, :] = o.reshape(sub, HEAD_DIM)
            lse_ref[br, rows, :] = jnp.broadcast_to(lse, (nb, w, HEAD_DIM)).reshape(sub, HEAD_DIM)

    ch = 256
    for c in range(mid):
        def combine(t, carry, c=c):
            start = pl.multiple_of(t * ch, ch)
            mid_rows = pl.ds(c * quarter + start, ch)
            seq_rows = pl.ds(start * mid + c, ch, stride=mid)
            rows = [seq_rows] + [mid_rows] * (nbr - 1)
            ls = [lse_ref[br, rows[br], :] for br in range(nbr)]
            m = functools.reduce(jnp.maximum, ls)
            ws = [jnp.exp2(l - m) for l in ls]
            tot = functools.reduce(lambda x, y: x + y, ws)
            acc = functools.reduce(lambda x, y: x + y, [wt * num_ref[br, rows[br], :] for br, wt in enumerate(ws)])
            stage_ref[0, seq_rows, :] = acc / tot
            return carry

        lax.fori_loop(0, quarter // ch, combine, 0)
    o_ref[0] = stage_ref[0].astype(o_ref.dtype)


def _dilated(qkv, rel_bias, *, batch, seq, head0=0, n_heads=N_HEADS_A):
    nh_all = qkv.shape[0] // 3
    view = qkv.reshape(3 * nh_all, batch, seq, HEAD_DIM)
    nbr = len(DILATIONS)
    return pl.pallas_call(
        functools.partial(_dilated_kernel, seq=seq, head0=head0),
        grid=(n_heads, batch),
        in_specs=[pl.BlockSpec(memory_space=pltpu.SMEM)] + [
            pl.BlockSpec((1, 1, seq, HEAD_DIM), lambda h, b, j=j: (j * nh_all + head0 + h, b, 0, 0))
            for j in range(3)],
        out_specs=pl.BlockSpec((1, seq, HEAD_DIM), lambda h, b: (b, 0, h)),
        out_shape=jax.ShapeDtypeStruct((batch, seq, n_heads * HEAD_DIM), BF16),
        scratch_shapes=[
            pltpu.VMEM((nbr, DIL_W, 2 * DIL_W), F32),
            pltpu.VMEM((3, seq, HEAD_DIM), F32),
            pltpu.VMEM((3, DIL_MID, seq // DIL_MID, HEAD_DIM), F32),
            pltpu.VMEM((nbr, seq, HEAD_DIM), F32),
            pltpu.VMEM((nbr, seq, HEAD_DIM), F32),
        ],
        compiler_params=_cparams(("arbitrary", "arbitrary")),
        name="dilated",
    )(rel_bias, view, view, view)


def _moba_kernel(tab_ref, q_ref, k_ref, v_ref, o_ref, bias_ref, qa_ref, ka_ref, va_ref, *, seq, head0):
    blk = MOBA_BLOCK
    nblk = seq // blk
    hd = HEAD_DIM
    h = pl.program_id(0)
    b = pl.program_id(1)

    @pl.when(b == 0)
    def _():
        r = lax.broadcasted_iota(jnp.int32, (blk, blk), 0)
        c = lax.broadcasted_iota(jnp.int32, (blk, blk), 1)

        for dlt in range(nblk):
            dist = dlt * blk + r - c
            bias = _bucket_bias(dist, tab_ref, head0 + h, max(dlt * blk - (blk - 1), 0), dlt * blk + blk - 1)
            bias_ref[dlt] = jnp.where(dist >= 0, bias * LOG2E, NEG)

    @pl.when((h == 0) & (b == 0))
    def _():
        key_blk = lax.broadcasted_iota(jnp.int32, (seq, hd), 0) // blk
        lane = lax.broadcasted_iota(jnp.int32, (seq, hd), 1)
        ka_ref[:, hd:] = jnp.where(lane == key_blk, MOBA_MASK, 0.0).astype(BF16)
        va_ref[:, hd:] = jnp.where(lane == 0, 1.0, 0.0).astype(BF16)

    q = q_ref[0, 0]
    k = k_ref[0, 0]
    qa_ref[:, :hd] = q
    ka_ref[:, :hd] = k
    va_ref[:, :hd] = v_ref[0, 0]

    kbar = jnp.sum(k.astype(F32).reshape(nblk, blk, hd), axis=1) * (1.0 / blk)
    k1 = kbar.astype(BF16)
    r1 = kbar - k1.astype(F32)
    k2 = r1.astype(BF16)
    k3 = (r1 - k2.astype(F32)).astype(BF16)
    gate = _dot_nt(k1, q) + _dot_nt(k2, q) + _dot_nt(k3, q)
    n_idx = lax.broadcasted_iota(jnp.int32, gate.shape, 0)
    own = lax.broadcasted_iota(jnp.int32, gate.shape, 1) // blk
    gate = jnp.where(n_idx < own, gate, -jnp.inf)
    skip = jnp.where(n_idx == own, 0.0, 1.0)
    for _ in range(min(MOBA_TOPK, nblk)):
        best = jnp.max(gate, axis=0, keepdims=True)
        idx = jnp.min(jnp.where(gate == best, n_idx, nblk), axis=0, keepdims=True)
        skip = jnp.where((n_idx == idx) & (best > -jnp.inf), 0.0, skip)
        gate = jnp.where(n_idx == idx, -jnp.inf, gate)
    skip = jnp.concatenate([skip, jnp.zeros((hd - nblk, seq), F32)], axis=0)
    qa_ref[:, hd:] = skip.T.astype(BF16)

    for i in range(nblk):
        width = (i + 1) * blk
        rows = slice(i * blk, (i + 1) * blk)
        s = _dot_nt(qa_ref[rows, :], ka_ref[:width, :])
        s = s + jnp.concatenate([bias_ref[i - n] for n in range(i + 1)], axis=1)
        m = jnp.max(s, axis=-1, keepdims=True)
        p = jnp.exp2(s - m).astype(BF16)
        o = _dot(p, va_ref[:width, :])
        o_ref[0, rows, :] = (o[:, :hd] / o[:, hd:hd + 1]).astype(o_ref.dtype)


def _moba(qkv, rel_bias, *, batch, seq, head0=N_HEADS_A, n_heads=N_HEADS - N_HEADS_A):
    nh_all = qkv.shape[0] // 3
    blk = MOBA_BLOCK
    nblk = seq // blk
    assert nblk <= HEAD_DIM
    view = qkv.reshape(3 * nh_all, batch, seq, HEAD_DIM)
    return pl.pallas_call(
        functools.partial(_moba_kernel, seq=seq, head0=head0),
        grid=(n_heads, batch),
        in_specs=[pl.BlockSpec(memory_space=pltpu.SMEM)] + [
            pl.BlockSpec((1, 1, seq, HEAD_DIM), lambda h, b, j=j: (j * nh_all + head0 + h, b, 0, 0))
            for j in range(3)],
        out_specs=pl.BlockSpec((1, seq, HEAD_DIM), lambda h, b: (b, 0, h)),
        out_shape=jax.ShapeDtypeStruct((batch, seq, n_heads * HEAD_DIM), BF16),
        scratch_shapes=[
            pltpu.VMEM((nblk, blk, blk), F32),
            pltpu.VMEM((seq, 2 * HEAD_DIM), BF16),
            pltpu.VMEM((seq, 2 * HEAD_DIM), BF16),
            pltpu.VMEM((seq, 2 * HEAD_DIM), BF16),
        ],
        compiler_params=_cparams(("arbitrary", "arbitrary")),
        name="moba",
    )(rel_bias, view, view, view)


def _sb_kernel(q_ref, k_ref, v_ref, o_ref, c_ref, acc_ref):
    t = SB_T
    i = pl.program_id(2)
    rows = lax.broadcasted_iota(jnp.int32, (t, t), 0)
    cols = lax.broadcasted_iota(jnp.int32, (t, t), 1)
    causal = cols < rows
    suffix = jnp.where(rows > cols, 1.0, 0.0).astype(BF16)
    sign_bit = jnp.uint32(0x80000000)

    def tile(q, krows, hh, keep):
        z = _dot_nt(q, k_ref[hh, 0, krows, :])
        neg_abs = lax.bitcast_convert_type(lax.bitcast_convert_type(z, jnp.uint32) | sign_bit, F32)
        sp = jnp.maximum(z, 0.0) + jnp.log(1.0 + jnp.exp2(neg_abs)) * LOG2E
        nk = sp if keep is None else jnp.where(keep, sp, 0.0)
        logw = (z - sp) - _dot(nk.astype(BF16), suffix)
        if keep is not None:
            logw = jnp.where(keep, logw, -jnp.inf)
        return nk, logw

    has_prev = i > 0
    own_rows = pl.ds(pl.multiple_of(i * t, t), t)
    prev_rows = pl.ds(pl.multiple_of(jnp.maximum(i - 1, 0) * t, t), t)
    for hh in range(SB_HEADS):
        q = q_ref[hh, 0]
        nk_a, lw_a = tile(q, own_rows, hh, causal)
        nk_b, lw_b = tile(q, prev_rows, hh, None)
        c_a = jnp.sum(nk_a, axis=-1, keepdims=True)
        p_a = jnp.exp2(lw_a).astype(BF16)
        p_b = jnp.exp2(jnp.where(has_prev, lw_b - c_a, -jnp.inf)).astype(BF16)
        acc_ref[hh] = _dot(p_a, v_ref[hh, 0, own_rows, :]) + _dot(p_b, v_ref[hh, 0, prev_rows, :])
        c_ref[hh] = c_a + jnp.where(has_prev, jnp.sum(nk_b, axis=-1, keepdims=True), 0.0)

    def cond(state):
        j, cmin = state
        return (j >= 0) & (cmin < SB_DEAD)

    def body(state):
        j, _ = state
        krows = pl.ds(pl.multiple_of(j * t, t), t)
        for hh in range(SB_HEADS):
            nk, lw = tile(q_ref[hh, 0], krows, hh, None)
            acc_ref[hh] += _dot(jnp.exp2(lw - c_ref[hh]).astype(BF16), v_ref[hh, 0, krows, :])
            c_ref[hh] += jnp.sum(nk, axis=-1, keepdims=True)
        return j - 1, jnp.min(c_ref[...])

    lax.while_loop(cond, body, (i - 2, jnp.min(c_ref[...])))
    o_ref[0] = jnp.concatenate([acc_ref[hh] for hh in range(SB_HEADS)], axis=1).astype(o_ref.dtype)


def _stickbreaking(qkv, *, batch, seq):
    nh = qkv.shape[0] // 3
    view = qkv.reshape(3 * nh, batch, seq, HEAD_DIM)
    g = SB_HEADS
    assert nh % g == 0
    return pl.pallas_call(
        _sb_kernel,
        grid=(batch, nh // g, seq // SB_T),
        in_specs=[
            pl.BlockSpec((g, 1, SB_T, HEAD_DIM), lambda b, h, i: (h, b, i, 0)),
            pl.BlockSpec((g, 1, seq, HEAD_DIM), lambda b, h, i: (nh // g + h, b, 0, 0)),
            pl.BlockSpec((g, 1, seq, HEAD_DIM), lambda b, h, i: (2 * (nh // g) + h, b, 0, 0)),
        ],
        out_specs=pl.BlockSpec((1, SB_T, g * HEAD_DIM), lambda b, h, i: (b, i, h)),
        out_shape=jax.ShapeDtypeStruct((batch, seq, nh * HEAD_DIM), BF16),
        scratch_shapes=[pltpu.VMEM((g, SB_T, 1), F32), pltpu.VMEM((g, SB_T, HEAD_DIM), F32)],
        compiler_params=_cparams(("parallel", "parallel", "arbitrary")),
        name="stickbreaking",
    )(view, view, view)


def kernel(x, ln_gains, ffn_w_gate, ffn_w_up, ffn_w_down, w_qkv_even, w_out_even, w_qkv_odd, w_out_odd,
           rel_bias, final_gain):
    batch, seq, d = x.shape
    depth = ln_gains.shape[0]
    m = batch * seq
    wg, wu, wd = (w.astype(BF16) for w in (ffn_w_gate, ffn_w_up, ffn_w_down))
    wqkv = (w_qkv_even.astype(BF16), w_qkv_odd.astype(BF16))
    wout = (w_out_even.astype(BF16), w_out_odd.astype(BF16))
    rel_bias = rel_bias.astype(F32)

    h = x.reshape(m, d)
    for i in range(depth):
        h = _ffn(h, ln_gains[i, 0], wg, wu, wd, i, 0)
        qkv = _qkv(h, ln_gains[i, 1], wqkv[i % 2][i // 2])
        if i % 2 == 0:
            parts = [_dilated(qkv, rel_bias, batch=batch, seq=seq), _moba(qkv, rel_bias, batch=batch, seq=seq)]
        else:
            parts = [_stickbreaking(qkv, batch=batch, seq=seq)]
        parts = [p.reshape(m, p.shape[-1]) for p in parts]
        h = _oproj(h, parts, wout[i % 2][i // 2])
        last = i == depth - 1
        h = _ffn(h, ln_gains[i, 2], wg, wu, wd, i, 1, final_gain if last else None)
    return h.reshape(batch, seq, d)
```

```python
import functools
import math

import jax
import jax.numpy as jnp
from jax import lax
from jax.experimental import pallas as pl
from jax.experimental.pallas import tpu as pltpu

F32 = jnp.float32
BF16 = jnp.bfloat16

HEAD_DIM = 128
N_HEADS = 16
N_HEADS_A = 8
DILATIONS = (1, 4, 16)
DIL_W = 128
DIL_MID = 4
MOBA_BLOCK = 256
MOBA_TOPK = 3
LOG2E = math.log2(math.e)
MOBA_MASK = -(2.0 ** 100)
REL_BUCKETS = 32
REL_MAX_DIST = 2048
FFN_HALF = 0.5
RMS_EPS = 1e-6
NEG = -1e30
SB_T = 256
SB_HEADS = 8
SB_DEAD = 152.0
V7X_VMEM_LIMIT = 56 * 1024 * 1024


def _cparams(sem):
    return pltpu.CompilerParams(dimension_semantics=sem, vmem_limit_bytes=V7X_VMEM_LIMIT)


def _rms(x, g):
    ms = jnp.mean(x * x, axis=-1, keepdims=True)
    return x * lax.rsqrt(ms + RMS_EPS) * g


def _dot(a, b):
    return jnp.dot(a, b, preferred_element_type=F32)


def _serpentine(outer, inner, n_inner):
    return jnp.where(outer % 2 == 0, inner, n_inner - 1 - inner)


def _dot_nt(a, b):
    return lax.dot_general(a, b, (((1,), (1,)), ((), ())), preferred_element_type=F32)


def _ffn_kernel(h_ref, g_ref, wg_ref, wu_ref, wd_ref, *rest, nf, tail, tail_step, dn, final):
    if final:
        fg_ref, o_ref, n_ref = rest
    else:
        o_ref, n_ref = rest
    f = pl.program_id(1)
    tf = wg_ref.shape[1]
    d = o_ref.shape[1]

    def chunk(width, first, last):
        if first:
            n = _rms(h_ref[...], g_ref[...]).astype(BF16)
            n_ref[...] = n
        else:
            n = n_ref[...]
        gate = _dot(n, wg_ref[:, :width])
        up = _dot(n, wu_ref[:, :width])
        a = (gate * (1.0 / (1.0 + jnp.exp(-gate))) * up).astype(BF16)
        for c in range(d // dn):
            sl = slice(c * dn, (c + 1) * dn)
            acc = _dot(a, wd_ref[:width, sl])
            if not first:
                acc = o_ref[:, sl] + acc
            o_ref[:, sl] = h_ref[:, sl] + FFN_HALF * acc if last else acc
        if last and final:
            o_ref[...] = _rms(o_ref[...], fg_ref[...])

    variants = {}
    for step in range(nf):
        key = (tail if step == tail_step else tf, step == 0, step == nf - 1)
        variants.setdefault(key, []).append(step)
    for (width, first, last), steps in variants.items():
        hit = functools.reduce(jnp.logical_or, [f == s for s in steps])
        pl.when(hit)(functools.partial(chunk, width, first, last))


def _ffn(h, gain, wg, wu, wd, layer, half, final_gain=None, *, tm=512, tf=1024, dn=512):
    m, d = h.shape
    dff = wg.shape[-1]
    nf = pl.cdiv(dff, tf)
    tail = dff - (nf - 1) * tf
    tail_step = (nf - 1) // 2
    final = final_gain is not None

    def chunk(f):
        return jnp.where(f == tail_step, nf - 1, jnp.where(f < tail_step, f, f - 1))

    in_specs = [
        pl.BlockSpec((tm, d), lambda i, f: (i, 0)),
        pl.BlockSpec((1, d), lambda i, f: (0, 0)),
        pl.BlockSpec((None, None, d, tf), lambda i, f: (layer, half, 0, chunk(f))),
        pl.BlockSpec((None, None, d, tf), lambda i, f: (layer, half, 0, chunk(f))),
        pl.BlockSpec((None, None, tf, d), lambda i, f: (layer, half, chunk(f), 0)),
    ]
    args = [h, gain.reshape(1, d), wg, wu, wd]
    if final:
        in_specs.append(pl.BlockSpec((1, d), lambda i, f: (0, 0)))
        args.append(final_gain.reshape(1, d))
    return pl.pallas_call(
        functools.partial(_ffn_kernel, nf=nf, tail=tail, tail_step=tail_step, dn=dn, final=final),
        grid=(m // tm, nf),
        in_specs=in_specs,
        out_specs=pl.BlockSpec((tm, d), lambda i, f: (i, 0)),
        out_shape=jax.ShapeDtypeStruct((m, d), F32),
        scratch_shapes=[pltpu.VMEM((tm, d), BF16)],
        compiler_params=_cparams(("parallel", "arbitrary")),
        name="ffn_final" if final else "ffn",
    )(*args)


def _qkv_kernel(h_ref, g_ref, w_ref, o_ref, n_ref, *, q_blocks, n_blocks, scale):
    col = _serpentine(pl.program_id(0), pl.program_id(1), n_blocks)

    def project(first):
        if first:
            n = _rms(h_ref[...], g_ref[...]).astype(BF16)
            n_ref[...] = n
        else:
            n = n_ref[...]
        acc = _dot(n, w_ref[...].astype(BF16))
        acc = acc * jnp.where(col < q_blocks, scale, 1.0).astype(F32)
        for c in range(o_ref.shape[0]):
            o_ref[c] = acc[:, c * HEAD_DIM:(c + 1) * HEAD_DIM].astype(BF16)

    pl.when(pl.program_id(1) == 0)(functools.partial(project, True))
    pl.when(pl.program_id(1) > 0)(functools.partial(project, False))


def _qkv(h, gain, w, layer, *, tm=1024, tn=1024):
    m, d = h.shape
    n3 = w.shape[-1]
    hpb = tn // HEAD_DIM
    nj = n3 // tn
    return pl.pallas_call(
        functools.partial(_qkv_kernel, q_blocks=(n3 // 3) // tn, n_blocks=nj, scale=HEAD_DIM ** -0.5 * LOG2E),
        grid=(m // tm, nj),
        in_specs=[
            pl.BlockSpec((tm, d), lambda i, j: (i, 0)),
            pl.BlockSpec((1, d), lambda i, j: (0, 0)),
            pl.BlockSpec((None, d, tn), lambda i, j: (layer, 0, _serpentine(i, j, nj))),
        ],
        out_specs=pl.BlockSpec((hpb, tm, HEAD_DIM), lambda i, j: (_serpentine(i, j, nj), i, 0)),
        out_shape=jax.ShapeDtypeStruct((n3 // HEAD_DIM, m, HEAD_DIM), BF16),
        scratch_shapes=[pltpu.VMEM((tm, d), BF16)],
        compiler_params=_cparams(("parallel", "arbitrary")),
        name="qkv",
    )(h, gain.reshape(1, d), w)


def _oproj_kernel(h_ref, *rest, n_in):
    a_refs = rest[:n_in]
    w_ref = rest[n_in]
    o_ref = rest[n_in + 1]
    acc = h_ref[...]
    off = 0
    for a_ref in a_refs:
        k = a_ref.shape[1]
        acc = acc + _dot(a_ref[...], w_ref[off:off + k, :])
        off += k
    o_ref[...] = acc


def _oproj(h, attn_parts, w, *, tm=512):
    m, d = h.shape
    in_specs = [pl.BlockSpec((tm, d), lambda i: (i, 0))]
    for a in attn_parts:
        in_specs.append(pl.BlockSpec((tm, a.shape[1]), lambda i: (i, 0)))
    in_specs.append(pl.BlockSpec(w.shape, lambda i: (0, 0)))
    return pl.pallas_call(
        functools.partial(_oproj_kernel, n_in=len(attn_parts)),
        grid=(m // tm,),
        in_specs=in_specs,
        out_specs=pl.BlockSpec((tm, d), lambda i: (i, 0)),
        out_shape=jax.ShapeDtypeStruct((m, d), F32),
        compiler_params=_cparams(("parallel",)),
        name="oproj",
    )(h, *attn_parts, w)


def _bucket_of(dist):
    max_exact = REL_BUCKETS // 2
    d = max(dist, 0)
    if d < max_exact:
        return d
    large = max_exact + int(math.log(d / max_exact) / math.log(REL_MAX_DIST / max_exact) * (REL_BUCKETS - max_exact))
    return min(large, REL_BUCKETS - 1)


def _bucket_bias(dist, tab_ref, head, dist_min, dist_max):
    max_exact = REL_BUCKETS // 2
    d = jnp.maximum(dist, 0)
    df = jnp.maximum(d, 1).astype(F32)
    large = max_exact + (jnp.log(df / max_exact) / math.log(REL_MAX_DIST / max_exact)
                         * (REL_BUCKETS - max_exact)).astype(jnp.int32)
    large = jnp.minimum(large, REL_BUCKETS - 1)
    bucket = jnp.where(d < max_exact, d, large)
    lo = max(_bucket_of(dist_min) - 1, 0)
    hi = min(_bucket_of(dist_max) + 1, REL_BUCKETS - 1)
    out = jnp.full(dist.shape, tab_ref[lo, head], F32)
    for b in range(lo + 1, hi + 1):
        out = jnp.where(bucket == b, tab_ref[b, head], out)
    return out


def _dilated_kernel(tab_ref, q_ref, k_ref, v_ref, o_ref, bias_ref, stage_ref, mid_ref, num_ref, lse_ref,
                    *, seq, head0):
    nbr = len(DILATIONS)
    h = pl.program_id(0)
    b = pl.program_id(1)
    w = DIL_W
    j = lax.broadcasted_iota(jnp.int32, (w, 2 * w), 1)

    @pl.when(b == 0)
    def _():
        i = lax.broadcasted_iota(jnp.int32, (w, 2 * w), 0)
        rel = w + i - j
        ok = (rel >= 0) & (rel <= w)
        for br, r in enumerate(DILATIONS):
            bias_ref[br] = jnp.where(ok, _bucket_bias(rel * r, tab_ref, head0 + h, 0, w * r) * LOG2E, NEG)

    no_prev = jnp.where(j < w, NEG, 0.0).astype(F32)

    mid = DIL_MID
    quarter = seq // mid
    assert DILATIONS[0] == 1 and all(r % mid == 0 for r in DILATIONS[1:])
    for t, ref in enumerate((q_ref, k_ref, v_ref)):
        stage_ref[t] = ref[0, 0].astype(F32)
        for c in range(mid):
            mid_ref[t, c] = stage_ref[t, pl.ds(c, seq // mid, stride=mid), :]

    def subsequence(t, r, c):
        if r == mid:
            return mid_ref[t, c].astype(BF16)
        return mid_ref[t, c % mid, pl.ds(c // mid, seq // r, stride=r // mid), :].astype(BF16)

    for br, r in enumerate(DILATIONS):
        sub = seq // r
        nb = sub // w
        for c in range(r):
            if r == 1:
                rows = slice(None)
            elif r == mid:
                rows = slice(c * quarter, (c + 1) * quarter)
            else:
                rows = pl.ds((c % mid) * quarter + c // mid, sub, stride=r // mid)
            if r > 1:
                qc, kc, vc = (subsequence(t, r, c) for t in range(3))
            else:
                qc, kc, vc = q_ref[0, 0], k_ref[0, 0], v_ref[0, 0]

            def band(t):
                prev = jnp.concatenate([t[:w], t[:sub - w]], axis=0) if nb > 1 else t
                return jnp.concatenate([prev.reshape(nb, w, HEAD_DIM), t.reshape(nb, w, HEAD_DIM)], axis=1)

            s = lax.dot_general(qc.reshape(nb, w, HEAD_DIM), band(kc), (((2,), (2,)), ((0,), (0,))),
                                preferred_element_type=F32) + bias_ref[br][None]
            s = jnp.concatenate([s[:1] + no_prev[None], s[1:]], axis=0) if nb > 1 else s + no_prev[None]
            m = jnp.max(s, axis=-1, keepdims=True)
            p = jnp.exp2(s - m)
            den = jnp.sum(p, axis=-1, keepdims=True)
            o = lax.dot_general(p.astype(BF16), band(vc), (((2,), (1,)), ((0,), (0,))),
                                preferred_element_type=F32) / den
            lse = m + jnp.log(den) * LOG2E
            num_ref[br, rows, :] = o.reshape(sub, HEAD_DIM)
            lse_ref[br, rows, :] = jnp.broadcast_to(lse, (nb, w, HEAD_DIM)).reshape(sub, HEAD_DIM)

    ch = 256
    for c in range(mid):
        def combine(t, carry, c=c):
            start = pl.multiple_of(t * ch, ch)
            mid_rows = pl.ds(c * quarter + start, ch)
            seq_rows = pl.ds(start * mid + c, ch, stride=mid)
            rows = [seq_rows] + [mid_rows] * (nbr - 1)
            ls = [lse_ref[br, rows[br], :] for br in range(nbr)]
            m = functools.reduce(jnp.maximum, ls)
            ws = [jnp.exp2(l - m) for l in ls]
            tot = functools.reduce(lambda x, y: x + y, ws)
            acc = functools.reduce(lambda x, y: x + y, [wt * num_ref[br, rows[br], :] for br, wt in enumerate(ws)])
            stage_ref[0, seq_rows, :] = acc / tot
            return carry

        lax.fori_loop(0, quarter // ch, combine, 0)
    o_ref[0] = stage_ref[0].astype(o_ref.dtype)


def _dilated(qkv, rel_bias, *, batch, seq, head0=0, n_heads=N_HEADS_A):
    nh_all = qkv.shape[0] // 3
    view = qkv.reshape(3 * nh_all, batch, seq, HEAD_DIM)
    nbr = len(DILATIONS)
    return pl.pallas_call(
        functools.partial(_dilated_kernel, seq=seq, head0=head0),
        grid=(n_heads, batch),
        in_specs=[pl.BlockSpec(memory_space=pltpu.SMEM)] + [
            pl.BlockSpec((1, 1, seq, HEAD_DIM), lambda h, b, j=j: (j * nh_all + head0 + h, b, 0, 0))
            for j in range(3)],
        out_specs=pl.BlockSpec((1, seq, HEAD_DIM), lambda h, b: (b, 0, h)),
        out_shape=jax.ShapeDtypeStruct((batch, seq, n_heads * HEAD_DIM), BF16),
        scratch_shapes=[
            pltpu.VMEM((nbr, DIL_W, 2 * DIL_W), F32),
            pltpu.VMEM((3, seq, HEAD_DIM), F32),
            pltpu.VMEM((3, DIL_MID, seq // DIL_MID, HEAD_DIM), F32),
            pltpu.VMEM((nbr, seq, HEAD_DIM), F32),
            pltpu.VMEM((nbr, seq, HEAD_DIM), F32),
        ],
        compiler_params=_cparams(("arbitrary", "arbitrary")),
        name="dilated",
    )(rel_bias, view, view, view)


def _moba_kernel(tab_ref, q_ref, k_ref, v_ref, o_ref, bias_ref, qa_ref, ka_ref, va_ref, *, seq, head0):
    blk = MOBA_BLOCK
    nblk = seq // blk
    hd = HEAD_DIM
    h = pl.program_id(0)
    b = pl.program_id(1)

    @pl.when(b == 0)
    def _():
        r = lax.broadcasted_iota(jnp.int32, (blk, blk), 0)
        c = lax.broadcasted_iota(jnp.int32, (blk, blk), 1)

        for dlt in range(nblk):
            dist = dlt * blk + r - c
            bias = _bucket_bias(dist, tab_ref, head0 + h, max(dlt * blk - (blk - 1), 0), dlt * blk + blk - 1)
            bias_ref[dlt] = jnp.where(dist >= 0, bias * LOG2E, NEG)

    @pl.when((h == 0) & (b == 0))
    def _():
        key_blk = lax.broadcasted_iota(jnp.int32, (seq, hd), 0) // blk
        lane = lax.broadcasted_iota(jnp.int32, (seq, hd), 1)
        ka_ref[:, hd:] = jnp.where(lane == key_blk, MOBA_MASK, 0.0).astype(BF16)
        va_ref[:, hd:] = jnp.where(lane == 0, 1.0, 0.0).astype(BF16)

    q = q_ref[0, 0]
    k = k_ref[0, 0]
    qa_ref[:, :hd] = q
    ka_ref[:, :hd] = k
    va_ref[:, :hd] = v_ref[0, 0]

    kbar = jnp.sum(k.astype(F32).reshape(nblk, blk, hd), axis=1) * (1.0 / blk)
    k1 = kbar.astype(BF16)
    r1 = kbar - k1.astype(F32)
    k2 = r1.astype(BF16)
    k3 = (r1 - k2.astype(F32)).astype(BF16)
    gate = _dot_nt(k1, q) + _dot_nt(k2, q) + _dot_nt(k3, q)
    n_idx = lax.broadcasted_iota(jnp.int32, gate.shape, 0)
    own = lax.broadcasted_iota(jnp.int32, gate.shape, 1) // blk
    gate = jnp.where(n_idx < own, gate, -jnp.inf)
    skip = jnp.where(n_idx == own, 0.0, 1.0)
    for _ in range(min(MOBA_TOPK, nblk)):
        best = jnp.max(gate, axis=0, keepdims=True)
        idx = jnp.min(jnp.where(gate == best, n_idx, nblk), axis=0, keepdims=True)
        skip = jnp.where((n_idx == idx) & (best > -jnp.inf), 0.0, skip)
        gate = jnp.where(n_idx == idx, -jnp.inf, gate)
    skip = jnp.concatenate([skip, jnp.zeros((hd - nblk, seq), F32)], axis=0)
    qa_ref[:, hd:] = skip.T.astype(BF16)

    for i in range(nblk):
        width = (i + 1) * blk
        rows = slice(i * blk, (i + 1) * blk)
        s = _dot_nt(qa_ref[rows, :], ka_ref[:width, :])
        s = s + jnp.concatenate([bias_ref[i - n] for n in range(i + 1)], axis=1)
        m = jnp.max(s, axis=-1, keepdims=True)
        p = jnp.exp2(s - m).astype(BF16)
        o = _dot(p, va_ref[:width, :])
        o_ref[0, rows, :] = (o[:, :hd] / o[:, hd:hd + 1]).astype(o_ref.dtype)


def _moba(qkv, rel_bias, *, batch, seq, head0=N_HEADS_A, n_heads=N_HEADS - N_HEADS_A):
    nh_all = qkv.shape[0] // 3
    blk = MOBA_BLOCK
    nblk = seq // blk
    assert nblk <= HEAD_DIM
    view = qkv.reshape(3 * nh_all, batch, seq, HEAD_DIM)
    return pl.pallas_call(
        functools.partial(_moba_kernel, seq=seq, head0=head0),
        grid=(n_heads, batch),
        in_specs=[pl.BlockSpec(memory_space=pltpu.SMEM)] + [
            pl.BlockSpec((1, 1, seq, HEAD_DIM), lambda h, b, j=j: (j * nh_all + head0 + h, b, 0, 0))
            for j in range(3)],
        out_specs=pl.BlockSpec((1, seq, HEAD_DIM), lambda h, b: (b, 0, h)),
        out_shape=jax.ShapeDtypeStruct((batch, seq, n_heads * HEAD_DIM), BF16),
        scratch_shapes=[
            pltpu.VMEM((nblk, blk, blk), F32),
            pltpu.VMEM((seq, 2 * HEAD_DIM), BF16),
            pltpu.VMEM((seq, 2 * HEAD_DIM), BF16),
            pltpu.VMEM((seq, 2 * HEAD_DIM), BF16),
        ],
        compiler_params=_cparams(("arbitrary", "arbitrary")),
        name="moba",
    )(rel_bias, view, view, view)


def _sb_kernel(q_ref, k_ref, v_ref, o_ref, c_ref, acc_ref):
    t = SB_T
    i = pl.program_id(2)
    rows = lax.broadcasted_iota(jnp.int32, (t, t), 0)
    cols = lax.broadcasted_iota(jnp.int32, (t, t), 1)
    causal = cols < rows
    suffix = jnp.where(rows > cols, 1.0, 0.0).astype(BF16)
    sign_bit = jnp.uint32(0x80000000)

    def tile(q, krows, hh, keep):
        z = _dot_nt(q, k_ref[hh, 0, krows, :])
        neg_abs = lax.bitcast_convert_type(lax.bitcast_convert_type(z, jnp.uint32) | sign_bit, F32)
        sp = jnp.maximum(z, 0.0) + jnp.log(1.0 + jnp.exp2(neg_abs)) * LOG2E
        nk = sp if keep is None else jnp.where(keep, sp, 0.0)
        logw = (z - sp) - _dot(nk.astype(BF16), suffix)
        if keep is not None:
            logw = jnp.where(keep, logw, -jnp.inf)
        return nk, logw

    has_prev = i > 0
    own_rows = pl.ds(pl.multiple_of(i * t, t), t)
    prev_rows = pl.ds(pl.multiple_of(jnp.maximum(i - 1, 0) * t, t), t)
    for hh in range(SB_HEADS):
        q = q_ref[hh, 0]
        nk_a, lw_a = tile(q, own_rows, hh, causal)
        nk_b, lw_b = tile(q, prev_rows, hh, None)
        c_a = jnp.sum(nk_a, axis=-1, keepdims=True)
        p_a = jnp.exp2(lw_a).astype(BF16)
        p_b = jnp.exp2(jnp.where(has_prev, lw_b - c_a, -jnp.inf)).astype(BF16)
        acc_ref[hh] = _dot(p_a, v_ref[hh, 0, own_rows, :]) + _dot(p_b, v_ref[hh, 0, prev_rows, :])
        c_ref[hh] = c_a + jnp.where(has_prev, jnp.sum(nk_b, axis=-1, keepdims=True), 0.0)

    def cond(state):
        j, cmin = state
        return (j >= 0) & (cmin < SB_DEAD)

    def body(state):
        j, _ = state
        krows = pl.ds(pl.multiple_of(j * t, t), t)
        for hh in range(SB_HEADS):
            nk, lw = tile(q_ref[hh, 0], krows, hh, None)
            acc_ref[hh] += _dot(jnp.exp2(lw - c_ref[hh]).astype(BF16), v_ref[hh, 0, krows, :])
            c_ref[hh] += jnp.sum(nk, axis=-1, keepdims=True)
        return j - 1, jnp.min(c_ref[...])

    lax.while_loop(cond, body, (i - 2, jnp.min(c_ref[...])))
    o_ref[0] = jnp.concatenate([acc_ref[hh] for hh in range(SB_HEADS)], axis=1).astype(o_ref.dtype)


def _stickbreaking(qkv, *, batch, seq):
    nh = qkv.shape[0] // 3
    view = qkv.reshape(3 * nh, batch, seq, HEAD_DIM)
    g = SB_HEADS
    assert nh % g == 0
    return pl.pallas_call(
        _sb_kernel,
        grid=(batch, nh // g, seq // SB_T),
        in_specs=[
            pl.BlockSpec((g, 1, SB_T, HEAD_DIM), lambda b, h, i: (h, b, i, 0)),
            pl.BlockSpec((g, 1, seq, HEAD_DIM), lambda b, h, i: (nh // g + h, b, 0, 0)),
            pl.BlockSpec((g, 1, seq, HEAD_DIM), lambda b, h, i: (2 * (nh // g) + h, b, 0, 0)),
        ],
        out_specs=pl.BlockSpec((1, SB_T, g * HEAD_DIM), lambda b, h, i: (b, i, h)),
        out_shape=jax.ShapeDtypeStruct((batch, seq, nh * HEAD_DIM), BF16),
        scratch_shapes=[pltpu.VMEM((g, SB_T, 1), F32), pltpu.VMEM((g, SB_T, HEAD_DIM), F32)],
        compiler_params=_cparams(("parallel", "parallel", "arbitrary")),
        name="stickbreaking",
    )(view, view, view)


def kernel(x, ln_gains, ffn_w_gate, ffn_w_up, ffn_w_down, w_qkv_even, w_out_even, w_qkv_odd, w_out_odd,
           rel_bias, final_gain):
    batch, seq, d = x.shape
    depth = ln_gains.shape[0]
    m = batch * seq
    wg, wu, wd = (w.astype(BF16) for w in (ffn_w_gate, ffn_w_up, ffn_w_down))
    wqkv = (w_qkv_even, w_qkv_odd)
    wout = (w_out_even.astype(BF16), w_out_odd.astype(BF16))
    rel_bias = rel_bias.astype(F32)

    h = x.reshape(m, d)
    for i in range(depth):
        h = _ffn(h, ln_gains[i, 0], wg, wu, wd, i, 0)
        qkv = _qkv(h, ln_gains[i, 1], wqkv[i % 2], i // 2)
        if i % 2 == 0:
            parts = [_dilated(qkv, rel_bias, batch=batch, seq=seq), _moba(qkv, rel_bias, batch=batch, seq=seq)]
        else:
            parts = [_stickbreaking(qkv, batch=batch, seq=seq)]
        parts = [p.reshape(m, p.shape[-1]) for p in parts]
        h = _oproj(h, parts, wout[i % 2][i // 2])
        last = i == depth - 1
        h = _ffn(h, ln_gains[i, 2], wg, wu, wd, i, 1, final_gain if last else None)
    return h.reshape(batch, seq, d)
```

```python
import functools
import math

import jax
import jax.numpy as jnp
from jax import lax
from jax.experimental import pallas as pl
from jax.experimental.pallas import tpu as pltpu

F32 = jnp.float32
BF16 = jnp.bfloat16

HEAD_DIM = 128
N_HEADS = 16
N_HEADS_A = 8
DILATIONS = (1, 4, 16)
DIL_W = 128
DIL_MID = 4
MOBA_BLOCK = 256
MOBA_TOPK = 3
LOG2E = math.log2(math.e)
MOBA_MASK = -(2.0 ** 100)
REL_BUCKETS = 32
REL_MAX_DIST = 2048
FFN_HALF = 0.5
RMS_EPS = 1e-6
NEG = -1e30
SB_T = 256
SB_HEADS = 8
SB_DEAD = 152.0
V7X_VMEM_LIMIT = 56 * 1024 * 1024


def _cparams(sem):
    return pltpu.CompilerParams(dimension_semantics=sem, vmem_limit_bytes=V7X_VMEM_LIMIT)


def _rms(x, g):
    ms = jnp.mean(x * x, axis=-1, keepdims=True)
    return x * lax.rsqrt(ms + RMS_EPS) * g


def _dot(a, b):
    return jnp.dot(a, b, preferred_element_type=F32)


def _serpentine(outer, inner, n_inner):
    return jnp.where(outer % 2 == 0, inner, n_inner - 1 - inner)


def _dot_nt(a, b):
    return lax.dot_general(a, b, (((1,), (1,)), ((), ())), preferred_element_type=F32)


def _ffn_kernel(h_ref, g_ref, wg_ref, wu_ref, wd_ref, *rest, nf, tail, tail_step, dn, final):
    if final:
        fg_ref, o_ref, n_ref = rest
    else:
        o_ref, n_ref = rest
    f = pl.program_id(1)
    tf = wg_ref.shape[1]
    d = o_ref.shape[1]

    def chunk(width, first, last):
        if first:
            n = _rms(h_ref[...], g_ref[...]).astype(BF16)
            n_ref[...] = n
        else:
            n = n_ref[...]
        gate = _dot(n, wg_ref[:, :width])
        up = _dot(n, wu_ref[:, :width])
        a = (gate * (1.0 / (1.0 + jnp.exp(-gate))) * up).astype(BF16)
        for c in range(d // dn):
            sl = slice(c * dn, (c + 1) * dn)
            acc = _dot(a, wd_ref[:width, sl])
            if not first:
                acc = o_ref[:, sl] + acc
            o_ref[:, sl] = h_ref[:, sl] + FFN_HALF * acc if last else acc
        if last and final:
            o_ref[...] = _rms(o_ref[...], fg_ref[...])

    variants = {}
    for step in range(nf):
        key = (tail if step == tail_step else tf, step == 0, step == nf - 1)
        variants.setdefault(key, []).append(step)
    for (width, first, last), steps in variants.items():
        hit = functools.reduce(jnp.logical_or, [f == s for s in steps])
        pl.when(hit)(functools.partial(chunk, width, first, last))


def _ffn(h, gain, wg, wu, wd, layer, half, final_gain=None, *, tm=512, tf=1024, dn=512):
    m, d = h.shape
    dff = wg.shape[-1]
    nf = pl.cdiv(dff, tf)
    tail = dff - (nf - 1) * tf
    tail_step = (nf - 1) // 2
    final = final_gain is not None

    def chunk(f):
        return jnp.where(f == tail_step, nf - 1, jnp.where(f < tail_step, f, f - 1))

    in_specs = [
        pl.BlockSpec((tm, d), lambda i, f: (i, 0)),
        pl.BlockSpec((1, d), lambda i, f: (0, 0)),
        pl.BlockSpec((None, None, d, tf), lambda i, f: (layer, half, 0, chunk(f))),
        pl.BlockSpec((None, None, d, tf), lambda i, f: (layer, half, 0, chunk(f))),
        pl.BlockSpec((None, None, tf, d), lambda i, f: (layer, half, chunk(f), 0)),
    ]
    args = [h, gain.reshape(1, d), wg, wu, wd]
    if final:
        in_specs.append(pl.BlockSpec((1, d), lambda i, f: (0, 0)))
        args.append(final_gain.reshape(1, d))
    return pl.pallas_call(
        functools.partial(_ffn_kernel, nf=nf, tail=tail, tail_step=tail_step, dn=dn, final=final),
        grid=(m // tm, nf),
        in_specs=in_specs,
        out_specs=pl.BlockSpec((tm, d), lambda i, f: (i, 0)),
        out_shape=jax.ShapeDtypeStruct((m, d), F32),
        scratch_shapes=[pltpu.VMEM((tm, d), BF16)],
        compiler_params=_cparams(("parallel", "arbitrary")),
        name="ffn_final" if final else "ffn",
    )(*args)


def _qkv_kernel(h_ref, g_ref, w_ref, o_ref, n_ref, *, q_blocks, n_blocks, scale):
    col = _serpentine(pl.program_id(0), pl.program_id(1), n_blocks)

    def project(first):
        if first:
            n = _rms(h_ref[...], g_ref[...]).astype(BF16)
            n_ref[...] = n
        else:
            n = n_ref[...]
        acc = _dot(n, w_ref[...].astype(BF16))
        acc = acc * jnp.where(col < q_blocks, scale, 1.0).astype(F32)
        for c in range(o_ref.shape[0]):
            o_ref[c] = acc[:, c * HEAD_DIM:(c + 1) * HEAD_DIM].astype(BF16)

    pl.when(pl.program_id(1) == 0)(functools.partial(project, True))
    pl.when(pl.program_id(1) > 0)(functools.partial(project, False))


def _qkv(h, gain, w, layer, *, tm=1024, tn=1024):
    m, d = h.shape
    n3 = w.shape[-1]
    hpb = tn // HEAD_DIM
    nj = n3 // tn
    return pl.pallas_call(
        functools.partial(_qkv_kernel, q_blocks=(n3 // 3) // tn, n_blocks=nj, scale=HEAD_DIM ** -0.5 * LOG2E),
        grid=(m // tm, nj),
        in_specs=[
            pl.BlockSpec((tm, d), lambda i, j: (i, 0)),
            pl.BlockSpec((1, d), lambda i, j: (0, 0)),
            pl.BlockSpec((None, d, tn), lambda i, j: (layer, 0, _serpentine(i, j, nj))),
        ],
        out_specs=pl.BlockSpec((hpb, tm, HEAD_DIM), lambda i, j: (_serpentine(i, j, nj), i, 0)),
        out_shape=jax.ShapeDtypeStruct((n3 // HEAD_DIM, m, HEAD_DIM), BF16),
        scratch_shapes=[pltpu.VMEM((tm, d), BF16)],
        compiler_params=_cparams(("parallel", "arbitrary")),
        name="qkv",
    )(h, gain.reshape(1, d), w)


def _oproj_kernel(h_ref, *rest, n_in):
    a_refs = rest[:n_in]
    w_ref, o_ref, wb_ref = rest[n_in:]

    @pl.when(pl.program_id(0) == 0)
    def _():
        wb_ref[...] = w_ref[...].astype(BF16)

    acc = h_ref[...]
    off = 0
    for a_ref in a_refs:
        k = a_ref.shape[1]
        acc = acc + _dot(a_ref[...], wb_ref[off:off + k, :])
        off += k
    o_ref[...] = acc


def _oproj(h, attn_parts, w, layer, *, tm=512):
    m, d = h.shape
    in_specs = [pl.BlockSpec((tm, d), lambda i: (i, 0))]
    for a in attn_parts:
        in_specs.append(pl.BlockSpec((tm, a.shape[1]), lambda i: (i, 0)))
    in_specs.append(pl.BlockSpec((None,) + w.shape[1:], lambda i: (layer, 0, 0), pipeline_mode=pl.Buffered(1)))
    return pl.pallas_call(
        functools.partial(_oproj_kernel, n_in=len(attn_parts)),
        grid=(m // tm,),
        in_specs=in_specs,
        out_specs=pl.BlockSpec((tm, d), lambda i: (i, 0)),
        out_shape=jax.ShapeDtypeStruct((m, d), F32),
        scratch_shapes=[pltpu.VMEM(w.shape[1:], BF16)],
        compiler_params=_cparams(("arbitrary",)),
        name="oproj",
    )(h, *attn_parts, w)


def _bucket_of(dist):
    max_exact = REL_BUCKETS // 2
    d = max(dist, 0)
    if d < max_exact:
        return d
    large = max_exact + int(math.log(d / max_exact) / math.log(REL_MAX_DIST / max_exact) * (REL_BUCKETS - max_exact))
    return min(large, REL_BUCKETS - 1)


def _bucket_bias(dist, tab_ref, head, dist_min, dist_max):
    max_exact = REL_BUCKETS // 2
    d = jnp.maximum(dist, 0)
    df = jnp.maximum(d, 1).astype(F32)
    large = max_exact + (jnp.log(df / max_exact) / math.log(REL_MAX_DIST / max_exact)
                         * (REL_BUCKETS - max_exact)).astype(jnp.int32)
    large = jnp.minimum(large, REL_BUCKETS - 1)
    bucket = jnp.where(d < max_exact, d, large)
    lo = max(_bucket_of(dist_min) - 1, 0)
    hi = min(_bucket_of(dist_max) + 1, REL_BUCKETS - 1)
    out = jnp.full(dist.shape, tab_ref[lo, head], F32)
    for b in range(lo + 1, hi + 1):
        out = jnp.where(bucket == b, tab_ref[b, head], out)
    return out


def _dilated_kernel(tab_ref, q_ref, k_ref, v_ref, o_ref, bias_ref, stage_ref, mid_ref, num_ref, lse_ref,
                    *, seq, head0):
    nbr = len(DILATIONS)
    h = pl.program_id(0)
    b = pl.program_id(1)
    w = DIL_W
    j = lax.broadcasted_iota(jnp.int32, (w, 2 * w), 1)

    @pl.when(b == 0)
    def _():
        i = lax.broadcasted_iota(jnp.int32, (w, 2 * w), 0)
        rel = w + i - j
        ok = (rel >= 0) & (rel <= w)
        for br, r in enumerate(DILATIONS):
            bias_ref[br] = jnp.where(ok, _bucket_bias(rel * r, tab_ref, head0 + h, 0, w * r) * LOG2E, NEG)

    no_prev = jnp.where(j < w, NEG, 0.0).astype(F32)

    mid = DIL_MID
    quarter = seq // mid
    assert DILATIONS[0] == 1 and all(r % mid == 0 for r in DILATIONS[1:])
    for t, ref in enumerate((q_ref, k_ref, v_ref)):
        stage_ref[t] = ref[0, 0].astype(F32)
        for c in range(mid):
            mid_ref[t, c] = stage_ref[t, pl.ds(c, seq // mid, stride=mid), :]

    def subsequence(t, r, c):
        if r == mid:
            return mid_ref[t, c].astype(BF16)
        return mid_ref[t, c % mid, pl.ds(c // mid, seq // r, stride=r // mid), :].astype(BF16)

    for br, r in enumerate(DILATIONS):
        sub = seq // r
        nb = sub // w
        for c in range(r):
            if r == 1:
                rows = slice(None)
            elif r == mid:
                rows = slice(c * quarter, (c + 1) * quarter)
            else:
                rows = pl.ds((c % mid) * quarter + c // mid, sub, stride=r // mid)
            if r > 1:
                qc, kc, vc = (subsequence(t, r, c) for t in range(3))
            else:
                qc, kc, vc = q_ref[0, 0], k_ref[0, 0], v_ref[0, 0]

            def band(t):
                prev = jnp.concatenate([t[:w], t[:sub - w]], axis=0) if nb > 1 else t
                return jnp.concatenate([prev.reshape(nb, w, HEAD_DIM), t.reshape(nb, w, HEAD_DIM)], axis=1)

            s = lax.dot_general(qc.reshape(nb, w, HEAD_DIM), band(kc), (((2,), (2,)), ((0,), (0,))),
                                preferred_element_type=F32) + bias_ref[br][None]
            s = jnp.concatenate([s[:1] + no_prev[None], s[1:]], axis=0) if nb > 1 else s + no_prev[None]
            m = jnp.max(s, axis=-1, keepdims=True)
            p = jnp.exp2(s - m)
            den = jnp.sum(p, axis=-1, keepdims=True)
            o = lax.dot_general(p.astype(BF16), band(vc), (((2,), (1,)), ((0,), (0,))),
                                preferred_element_type=F32) / den
            lse = m + jnp.log(den) * LOG2E
            num_ref[br, rows, :] = o.reshape(sub, HEAD_DIM)
            lse_ref[br, rows, :] = jnp.broadcast_to(lse, (nb, w, HEAD_DIM)).reshape(sub, HEAD_DIM)

    ch = 256
    for c in range(mid):
        def combine(t, carry, c=c):
            start = pl.multiple_of(t * ch, ch)
            mid_rows = pl.ds(c * quarter + start, ch)
            seq_rows = pl.ds(start * mid + c, ch, stride=mid)
            rows = [seq_rows] + [mid_rows] * (nbr - 1)
            ls = [lse_ref[br, rows[br], :] for br in range(nbr)]
            m = functools.reduce(jnp.maximum, ls)
            ws = [jnp.exp2(l - m) for l in ls]
            tot = functools.reduce(lambda x, y: x + y, ws)
            acc = functools.reduce(lambda x, y: x + y, [wt * num_ref[br, rows[br], :] for br, wt in enumerate(ws)])
            stage_ref[0, seq_rows, :] = acc / tot
            return carry

        lax.fori_loop(0, quarter // ch, combine, 0)
    o_ref[0] = stage_ref[0].astype(o_ref.dtype)


def _dilated(qkv, rel_bias, *, batch, seq, head0=0, n_heads=N_HEADS_A):
    nh_all = qkv.shape[0] // 3
    view = qkv.reshape(3 * nh_all, batch, seq, HEAD_DIM)
    nbr = len(DILATIONS)
    return pl.pallas_call(
        functools.partial(_dilated_kernel, seq=seq, head0=head0),
        grid=(n_heads, batch),
        in_specs=[pl.BlockSpec(memory_space=pltpu.SMEM)] + [
            pl.BlockSpec((1, 1, seq, HEAD_DIM), lambda h, b, j=j: (j * nh_all + head0 + h, b, 0, 0))
            for j in range(3)],
        out_specs=pl.BlockSpec((1, seq, HEAD_DIM), lambda h, b: (b, 0, h)),
        out_shape=jax.ShapeDtypeStruct((batch, seq, n_heads * HEAD_DIM), BF16),
        scratch_shapes=[
            pltpu.VMEM((nbr, DIL_W, 2 * DIL_W), F32),
            pltpu.VMEM((3, seq, HEAD_DIM), F32),
            pltpu.VMEM((3, DIL_MID, seq // DIL_MID, HEAD_DIM), F32),
            pltpu.VMEM((nbr, seq, HEAD_DIM), F32),
            pltpu.VMEM((nbr, seq, HEAD_DIM), F32),
        ],
        compiler_params=_cparams(("arbitrary", "arbitrary")),
        name="dilated",
    )(rel_bias, view, view, view)


def _moba_kernel(tab_ref, q_ref, k_ref, v_ref, o_ref, bias_ref, qa_ref, ka_ref, va_ref, *, seq, head0):
    blk = MOBA_BLOCK
    nblk = seq // blk
    hd = HEAD_DIM
    h = pl.program_id(0)
    b = pl.program_id(1)

    @pl.when(b == 0)
    def _():
        r = lax.broadcasted_iota(jnp.int32, (blk, blk), 0)
        c = lax.broadcasted_iota(jnp.int32, (blk, blk), 1)

        for dlt in range(nblk):
            dist = dlt * blk + r - c
            bias = _bucket_bias(dist, tab_ref, head0 + h, max(dlt * blk - (blk - 1), 0), dlt * blk + blk - 1)
            bias_ref[dlt] = jnp.where(dist >= 0, bias * LOG2E, NEG)

    @pl.when((h == 0) & (b == 0))
    def _():
        key_blk = lax.broadcasted_iota(jnp.int32, (seq, hd), 0) // blk
        lane = lax.broadcasted_iota(jnp.int32, (seq, hd), 1)
        ka_ref[:, hd:] = jnp.where(lane == key_blk, MOBA_MASK, 0.0).astype(BF16)
        va_ref[:, hd:] = jnp.where(lane == 0, 1.0, 0.0).astype(BF16)

    q = q_ref[0, 0]
    k = k_ref[0, 0]
    qa_ref[:, :hd] = q
    ka_ref[:, :hd] = k
    va_ref[:, :hd] = v_ref[0, 0]

    kbar = jnp.sum(k.astype(F32).reshape(nblk, blk, hd), axis=1) * (1.0 / blk)
    k1 = kbar.astype(BF16)
    r1 = kbar - k1.astype(F32)
    k2 = r1.astype(BF16)
    k3 = (r1 - k2.astype(F32)).astype(BF16)
    gate = _dot_nt(k1, q) + _dot_nt(k2, q) + _dot_nt(k3, q)
    n_idx = lax.broadcasted_iota(jnp.int32, gate.shape, 0)
    own = lax.broadcasted_iota(jnp.int32, gate.shape, 1) // blk
    gate = jnp.where(n_idx < own, gate, -jnp.inf)
    skip = jnp.where(n_idx == own, 0.0, 1.0)
    for _ in range(min(MOBA_TOPK, nblk)):
        best = jnp.max(gate, axis=0, keepdims=True)
        idx = jnp.min(jnp.where(gate == best, n_idx, nblk), axis=0, keepdims=True)
        skip = jnp.where((n_idx == idx) & (best > -jnp.inf), 0.0, skip)
        gate = jnp.where(n_idx == idx, -jnp.inf, gate)
    skip = jnp.concatenate([skip, jnp.zeros((hd - nblk, seq), F32)], axis=0)
    qa_ref[:, hd:] = skip.T.astype(BF16)

    for i in range(nblk):
        width = (i + 1) * blk
        rows = slice(i * blk, (i + 1) * blk)
        s = _dot_nt(qa_ref[rows, :], ka_ref[:width, :])
        s = s + jnp.concatenate([bias_ref[i - n] for n in range(i + 1)], axis=1)
        m = jnp.max(s, axis=-1, keepdims=True)
        p = jnp.exp2(s - m).astype(BF16)
        o = _dot(p, va_ref[:width, :])
        o_ref[0, rows, :] = (o[:, :hd] / o[:, hd:hd + 1]).astype(o_ref.dtype)


def _moba(qkv, rel_bias, *, batch, seq, head0=N_HEADS_A, n_heads=N_HEADS - N_HEADS_A):
    nh_all = qkv.shape[0] // 3
    blk = MOBA_BLOCK
    nblk = seq // blk
    assert nblk <= HEAD_DIM
    view = qkv.reshape(3 * nh_all, batch, seq, HEAD_DIM)
    return pl.pallas_call(
        functools.partial(_moba_kernel, seq=seq, head0=head0),
        grid=(n_heads, batch),
        in_specs=[pl.BlockSpec(memory_space=pltpu.SMEM)] + [
            pl.BlockSpec((1, 1, seq, HEAD_DIM), lambda h, b, j=j: (j * nh_all + head0 + h, b, 0, 0))
            for j in range(3)],
        out_specs=pl.BlockSpec((1, seq, HEAD_DIM), lambda h, b: (b, 0, h)),
        out_shape=jax.ShapeDtypeStruct((batch, seq, n_heads * HEAD_DIM), BF16),
        scratch_shapes=[
            pltpu.VMEM((nblk, blk, blk), F32),
            pltpu.VMEM((seq, 2 * HEAD_DIM), BF16),
            pltpu.VMEM((seq, 2 * HEAD_DIM), BF16),
            pltpu.VMEM((seq, 2 * HEAD_DIM), BF16),
        ],
        compiler_params=_cparams(("arbitrary", "arbitrary")),
        name="moba",
    )(rel_bias, view, view, view)


def _sb_kernel(q_ref, k_ref, v_ref, o_ref, c_ref, acc_ref):
    t = SB_T
    i = pl.program_id(2)
    rows = lax.broadcasted_iota(jnp.int32, (t, t), 0)
    cols = lax.broadcasted_iota(jnp.int32, (t, t), 1)
    causal = cols < rows
    suffix = jnp.where(rows > cols, 1.0, 0.0).astype(BF16)
    sign_bit = jnp.uint32(0x80000000)

    def tile(q, krows, hh, keep):
        z = _dot_nt(q, k_ref[hh, 0, krows, :])
        neg_abs = lax.bitcast_convert_type(lax.bitcast_convert_type(z, jnp.uint32) | sign_bit, F32)
        sp = jnp.maximum(z, 0.0) + jnp.log(1.0 + jnp.exp2(neg_abs)) * LOG2E
        nk = sp if keep is None else jnp.where(keep, sp, 0.0)
        logw = (z - sp) - _dot(nk.astype(BF16), suffix)
        if keep is not None:
            logw = jnp.where(keep, logw, -jnp.inf)
        return nk, logw

    has_prev = i > 0
    own_rows = pl.ds(pl.multiple_of(i * t, t), t)
    prev_rows = pl.ds(pl.multiple_of(jnp.maximum(i - 1, 0) * t, t), t)
    for hh in range(SB_HEADS):
        q = q_ref[hh, 0]
        nk_a, lw_a = tile(q, own_rows, hh, causal)
        nk_b, lw_b = tile(q, prev_rows, hh, None)
        c_a = jnp.sum(nk_a, axis=-1, keepdims=True)
        p_a = jnp.exp2(lw_a).astype(BF16)
        p_b = jnp.exp2(jnp.where(has_prev, lw_b - c_a, -jnp.inf)).astype(BF16)
        acc_ref[hh] = _dot(p_a, v_ref[hh, 0, own_rows, :]) + _dot(p_b, v_ref[hh, 0, prev_rows, :])
        c_ref[hh] = c_a + jnp.where(has_prev, jnp.sum(nk_b, axis=-1, keepdims=True), 0.0)

    def cond(state):
        j, cmin = state
        return (j >= 0) & (cmin < SB_DEAD)

    def body(state):
        j, _ = state
        krows = pl.ds(pl.multiple_of(j * t, t), t)
        for hh in range(SB_HEADS):
            nk, lw = tile(q_ref[hh, 0], krows, hh, None)
            acc_ref[hh] += _dot(jnp.exp2(lw - c_ref[hh]).astype(BF16), v_ref[hh, 0, krows, :])
            c_ref[hh] += jnp.sum(nk, axis=-1, keepdims=True)
        return j - 1, jnp.min(c_ref[...])

    lax.while_loop(cond, body, (i - 2, jnp.min(c_ref[...])))
    o_ref[0] = jnp.concatenate([acc_ref[hh] for hh in range(SB_HEADS)], axis=1).astype(o_ref.dtype)


def _stickbreaking(qkv, *, batch, seq):
    nh = qkv.shape[0] // 3
    view = qkv.reshape(3 * nh, batch, seq, HEAD_DIM)
    g = SB_HEADS
    assert nh % g == 0
    return pl.pallas_call(
        _sb_kernel,
        grid=(batch, nh // g, seq // SB_T),
        in_specs=[
            pl.BlockSpec((g, 1, SB_T, HEAD_DIM), lambda b, h, i: (h, b, i, 0)),
            pl.BlockSpec((g, 1, seq, HEAD_DIM), lambda b, h, i: (nh // g + h, b, 0, 0)),
            pl.BlockSpec((g, 1, seq, HEAD_DIM), lambda b, h, i: (2 * (nh // g) + h, b, 0, 0)),
        ],
        out_specs=pl.BlockSpec((1, SB_T, g * HEAD_DIM), lambda b, h, i: (b, i, h)),
        out_shape=jax.ShapeDtypeStruct((batch, seq, nh * HEAD_DIM), BF16),
        scratch_shapes=[pltpu.VMEM((g, SB_T, 1), F32), pltpu.VMEM((g, SB_T, HEAD_DIM), F32)],
        compiler_params=_cparams(("parallel", "parallel", "arbitrary")),
        name="stickbreaking",
    )(view, view, view)


def kernel(x, ln_gains, ffn_w_gate, ffn_w_up, ffn_w_down, w_qkv_even, w_out_even, w_qkv_odd, w_out_odd,
           rel_bias, final_gain):
    batch, seq, d = x.shape
    depth = ln_gains.shape[0]
    m = batch * seq
    wg, wu, wd = (w.astype(BF16) for w in (ffn_w_gate, ffn_w_up, ffn_w_down))
    wqkv = (w_qkv_even, w_qkv_odd)
    wout = (w_out_even, w_out_odd)
    rel_bias = rel_bias.astype(F32)

    h = x.reshape(m, d)
    for i in range(depth):
        h = _ffn(h, ln_gains[i, 0], wg, wu, wd, i, 0)
        qkv = _qkv(h, ln_gains[i, 1], wqkv[i % 2], i // 2)
        if i % 2 == 0:
            parts = [_dilated(qkv, rel_bias, batch=batch, seq=seq), _moba(qkv, rel_bias, batch=batch, seq=seq)]
        else:
            parts = [_stickbreaking(qkv, batch=batch, seq=seq)]
        parts = [p.reshape(m, p.shape[-1]) for p in parts]
        h = _oproj(h, parts, wout[i % 2], i // 2)
        last = i == depth - 1
        h = _ffn(h, ln_gains[i, 2], wg, wu, wd, i, 1, final_gain if last else None)
    return h.reshape(batch, seq, d)
```
